```python
import math
import jax, jax.numpy as jnp
from jax import lax
import numpy as np

D_MODEL = 1024
BATCH = 8
SEQ = 2048
DEPTH = 4
DEC_BATCH = 128
DEC_SEQ = 4
PAST_LEN = 16384
PAGE_SIZE = 128

N_MIXERS = 2
N_SSD_LAYERS = (DEPTH + 1) // 2
N_HGRN_LAYERS = DEPTH // 2
SSD_EXPAND = 2
SSD_D_INNER = SSD_EXPAND * D_MODEL
SSD_HEAD_DIM = 64
SSD_N_HEADS = SSD_D_INNER // SSD_HEAD_DIM
SSD_N_GROUPS = 4
SSD_HEADS_PER_GROUP = SSD_N_HEADS // SSD_N_GROUPS
SSD_D_STATE = 128
SSD_CONV_W = 4
SSD_GN = SSD_N_GROUPS * SSD_D_STATE
SSD_CONV_DIM = SSD_D_INNER + 2 * SSD_GN
SSD_IN_DIM = SSD_D_INNER + SSD_CONV_DIM + SSD_N_HEADS
SSD_CHUNK = 64
HGRN_EXPAND = 128
HGRN_N_HEADS = D_MODEL // HGRN_EXPAND
HGRN_DK = HGRN_EXPAND
HGRN_DV = D_MODEL // HGRN_N_HEADS
HGRN_F = HGRN_N_HEADS * HGRN_DK
HGRN_IN_DIM = 2 * HGRN_F + 2 * D_MODEL
HGRN_CHUNK = 32
LB_FLOOR = 1e-20
D_FF = 4 * D_MODEL
EPS = 1e-5

kernel_name = 'hybrid_ssd_hgrn2_decoder_step'


def rmsnorm(x, w):
    xf = x.astype(jnp.float32)
    y = xf * lax.rsqrt(jnp.mean(xf * xf, axis=-1, keepdims=True) + EPS)
    return (y * w.astype(jnp.float32)).astype(x.dtype)


def group_rmsnorm(x, w, n_groups):
    shp = x.shape
    xf = x.astype(jnp.float32).reshape(shp[:-1] + (n_groups, shp[-1] // n_groups))
    xf = xf * lax.rsqrt(jnp.mean(xf * xf, axis=-1, keepdims=True) + EPS)
    return xf.reshape(shp) * w.astype(jnp.float32)


def _pad_time(t, lp):
    pad = lp - t.shape[1]
    if pad == 0:
        return t
    return jnp.pad(t, [(0, 0), (0, pad)] + [(0, 0)] * (t.ndim - 2))


def _chunk(t, q):
    bt, lp = t.shape[:2]
    return jnp.moveaxis(t.reshape((bt, lp // q, q) + t.shape[2:]), 1, 0)


def _unchunk(t):
    nc, bt, q = t.shape[:3]
    return jnp.moveaxis(t, 0, 1).reshape((bt, nc * q) + t.shape[3:])


def _masked_decay(seg, mask):
    return jnp.where(mask, jnp.exp(jnp.where(mask, seg, 0.0)), 0.0)


def ssd_chunk_scan(x, dt, a, b, c, h0):
    bt, L = x.shape[:2]
    q = min(SSD_CHUNK, L)
    lp = -(-L // q) * q
    x, dt, b, c = (_pad_time(t, lp) for t in (x, dt, b, c))
    G, R, P, N = SSD_N_GROUPS, SSD_HEADS_PER_GROUP, SSD_HEAD_DIM, SSD_D_STATE
    x = x.reshape(bt, lp, G, R, P)
    dt = dt.reshape(bt, lp, G, R)
    a = a.reshape(G, R)
    causal = jnp.tril(jnp.ones((q, q), dtype=bool))[None, :, :, None, None]

    def step(h, inp):
        xq, dtq, bq, cq = inp
        cum = jnp.cumsum(dtq * a, axis=1)
        seg = cum[:, :, None] - cum[:, None, :]
        decay = _masked_decay(seg, causal)
        cb = jnp.einsum('bign,bjgn->bijg', cq, bq)
        w = cb[..., None] * decay * dtq[:, None]
        y = jnp.einsum('bijgr,bjgrp->bigrp', w, xq)
        y = y + jnp.einsum('bign,bgrpn->bigrp', cq, h) * jnp.exp(cum)[..., None]
        tail = jnp.exp(cum[:, -1:] - cum) * dtq
        h = jnp.exp(cum[:, -1])[..., None, None] * h + jnp.einsum('bjgr,bjgrp,bjgn->bgrpn', tail, xq, bq)
        return h, y

    h0 = h0.astype(jnp.float32).reshape(bt, G, R, P, N)
    hT, ys = lax.scan(step, h0, tuple(_chunk(t, q) for t in (x, dt, b, c)))
    y = _unchunk(ys)[:, :L].reshape(bt, L, SSD_N_HEADS, P)
    return y, hT.reshape(bt, SSD_N_HEADS, P, N)


def hgrn_chunk_scan(q, k, v, logf, s0):
    bt, L = q.shape[:2]
    cs = min(HGRN_CHUNK, L)
    lp = -(-L // cs) * cs
    q, k, v, logf = (_pad_time(t, lp) for t in (q, k, v, logf))
    causal = jnp.tril(jnp.ones((cs, cs), dtype=bool))[None, :, :, None, None]

    def step(s, inp):
        qc, kc, vc, lc = inp
        cum = jnp.cumsum(lc, axis=1)
        seg = cum[:, :, None] - cum[:, None]
        decay = _masked_decay(seg, causal)
        att = jnp.einsum('bihk,bjhk,bijhk->bijh', qc, kc, decay)
        o = jnp.einsum('bijh,bjhv->bihv', att, vc)
        o = o + jnp.einsum('bihk,bhkv->bihv', qc * jnp.exp(cum), s)
        kt = kc * jnp.exp(cum[:, -1:] - cum)
        s = jnp.exp(cum[:, -1])[..., None] * s + jnp.einsum('bjhk,bjhv->bhkv', kt, vc)
        return s, o

    sT, os_ = lax.scan(step, s0.astype(jnp.float32), tuple(_chunk(t, cs) for t in (q, k, v, logf)))
    return _unchunk(os_)[:, :L], sT


def ssd_mixer(u, conv_state, ssm_state, w_in, conv_w, conv_b, dt_bias, a_log, d_skip, norm_w, w_out):
    bt, L, _ = u.shape
    proj = u @ w_in
    z = proj[..., :SSD_D_INNER]
    xbc = proj[..., SSD_D_INNER:SSD_D_INNER + SSD_CONV_DIM]
    dt_raw = proj[..., SSD_D_INNER + SSD_CONV_DIM:]
    xbc_ext = jnp.concatenate([conv_state.astype(xbc.dtype), xbc], axis=1)
    new_conv = xbc_ext[:, xbc_ext.shape[1] - (SSD_CONV_W - 1):]
    conv = lax.conv_general_dilated(xbc_ext, conv_w[:, None, :].astype(xbc.dtype), window_strides=(1,),
                                    padding='VALID', dimension_numbers=('NWC', 'WIO', 'NWC'),
                                    feature_group_count=SSD_CONV_DIM)
    xbc = jax.nn.silu((conv + conv_b).astype(jnp.float32))
    xs = xbc[..., :SSD_D_INNER].reshape(bt, L, SSD_N_HEADS, SSD_HEAD_DIM)
    bs = xbc[..., SSD_D_INNER:SSD_D_INNER + SSD_GN].reshape(bt, L, SSD_N_GROUPS, SSD_D_STATE)
    cs = xbc[..., SSD_D_INNER + SSD_GN:].reshape(bt, L, SSD_N_GROUPS, SSD_D_STATE)
    dt = jax.nn.softplus(dt_raw.astype(jnp.float32) + dt_bias.astype(jnp.float32))
    a = -jnp.exp(a_log.astype(jnp.float32))
    y, h = ssd_chunk_scan(xs, dt, a, bs, cs, ssm_state)
    y = y + xs * d_skip.astype(jnp.float32)[:, None]
    y = y.reshape(bt, L, SSD_D_INNER) * jax.nn.silu(z.astype(jnp.float32))
    y = group_rmsnorm(y, norm_w, SSD_N_GROUPS).astype(u.dtype)
    return y @ w_out, new_conv, h


def hgrn_mixer(u, state, lb, w_in, norm_w, w_out):
    bt, L, _ = u.shape
    proj = (u @ w_in).astype(jnp.float32)
    q = jax.nn.silu(proj[..., :HGRN_F])
    fz = proj[..., HGRN_F:2 * HGRN_F]
    v = proj[..., 2 * HGRN_F:2 * HGRN_F + D_MODEL]
    g = proj[..., 2 * HGRN_F + D_MODEL:]
    lb = lb.astype(jnp.float32)
    logf = jnp.logaddexp(jax.nn.log_sigmoid(fz), jnp.log(jnp.maximum(lb, LB_FLOOR)) + jax.nn.log_sigmoid(-fz))
    k = (1.0 - lb) * jax.nn.sigmoid(-fz)
    hs = (bt, L, HGRN_N_HEADS)
    o, s = hgrn_chunk_scan(q.reshape(hs + (HGRN_DK,)), k.reshape(hs + (HGRN_DK,)),
                           v.reshape(hs + (HGRN_DV,)), logf.reshape(hs + (HGRN_DK,)), state)
    o = group_rmsnorm(o.reshape(bt, L, D_MODEL), norm_w, HGRN_N_HEADS) * jax.nn.silu(g)
    return o.astype(u.dtype) @ w_out, s


def squared_relu_mlp(u, w_up, w_down):
    return jnp.square(jax.nn.relu(u @ w_up)) @ w_down


def _trunk(x, conv_states, ssm_states, hgrn_states, hgrn_lb, norm_mix_w, norm_mlp_w, norm_f_w,
           ssd_w_in, ssd_conv_w, ssd_conv_b, ssd_dt_bias, ssd_a_log, ssd_d, ssd_norm_w, ssd_w_out,
           hgrn_w_in, hgrn_norm_w, hgrn_w_out, mlp_w_up, mlp_w_down):
    h = x
    new_conv, new_ssm, new_hgrn = [], [], []
    for layer in range(DEPTH):
        u = rmsnorm(h, norm_mix_w[layer])
        j = layer // N_MIXERS
        if layer % N_MIXERS == 0:
            out, cst, sst = ssd_mixer(u, conv_states[j], ssm_states[j], ssd_w_in[j], ssd_conv_w[j], ssd_conv_b[j],
                                      ssd_dt_bias[j], ssd_a_log[j], ssd_d[j], ssd_norm_w[j], ssd_w_out[j])
            new_conv.append(cst)
            new_ssm.append(sst)
        else:
            out, hst = hgrn_mixer(u, hgrn_states[j], hgrn_lb[j], hgrn_w_in[j], hgrn_norm_w[j], hgrn_w_out[j])
            new_hgrn.append(hst)
        h = h + out.astype(h.dtype)
        h = h + squared_relu_mlp(rmsnorm(h, norm_mlp_w[layer]), mlp_w_up[layer], mlp_w_down[layer]).astype(h.dtype)
    y = rmsnorm(h, norm_f_w)
    return y, jnp.stack(new_conv), jnp.stack(new_ssm), jnp.stack(new_hgrn)


def setup_inputs(seed: int = 0) -> dict:
    key = jax.random.key(seed)
    ks = iter(jax.random.split(key, 32))

    def nrm(shape, scale):
        return jax.random.normal(next(ks), shape, jnp.float32) * scale

    LA, LB = N_SSD_LAYERS, N_HGRN_LAYERS
    x_prompt = nrm((BATCH, SEQ, D_MODEL), 1.0)
    x_sample = nrm((DEC_BATCH, DEC_SEQ, D_MODEL), 1.0)
    state_ssd_conv = nrm((LA, DEC_BATCH, SSD_CONV_W - 1, SSD_CONV_DIM), 1.0)
    state_ssd_ssm = nrm((LA, DEC_BATCH, SSD_N_HEADS, SSD_HEAD_DIM, SSD_D_STATE), 0.5)
    state_hgrn = nrm((LB, DEC_BATCH, HGRN_N_HEADS, HGRN_DK, HGRN_DV), 0.5)
    norm_mix_w = 1.0 + nrm((DEPTH, D_MODEL), 0.02)
    norm_mlp_w = 1.0 + nrm((DEPTH, D_MODEL), 0.02)
    norm_f_w = 1.0 + nrm((D_MODEL,), 0.02)
    ssd_w_in = nrm((LA, D_MODEL, SSD_IN_DIM), D_MODEL ** -0.5)
    ssd_conv_w = nrm((LA, SSD_CONV_W, SSD_CONV_DIM), SSD_CONV_W ** -0.5)
    ssd_conv_b = nrm((LA, SSD_CONV_DIM), 0.02)
    dt0 = jnp.exp(jax.random.uniform(next(ks), (LA, SSD_N_HEADS), jnp.float32,
                                     minval=math.log(1e-3), maxval=math.log(1e-1)))
    ssd_dt_bias = dt0 + jnp.log(-jnp.expm1(-dt0))
    ssd_a_log = jnp.log(jax.random.uniform(next(ks), (LA, SSD_N_HEADS), jnp.float32, minval=1.0, maxval=16.0))
    ssd_d = 1.0 + nrm((LA, SSD_N_HEADS), 0.02)
    ssd_norm_w = 1.0 + nrm((LA, SSD_D_INNER), 0.02)
    ssd_w_out = nrm((LA, SSD_D_INNER, D_MODEL), SSD_D_INNER ** -0.5)
    hgrn_w_in = nrm((LB, D_MODEL, HGRN_IN_DIM), D_MODEL ** -0.5)
    hgrn_lb_raw = nrm((LB, HGRN_F), 1.0)
    hgrn_norm_w = 1.0 + nrm((LB, D_MODEL), 0.02)
    hgrn_w_out = nrm((LB, D_MODEL, D_MODEL), D_MODEL ** -0.5)
    mlp_w_up = nrm((DEPTH, D_MODEL, D_FF), D_MODEL ** -0.5)
    mlp_w_down = nrm((DEPTH, D_FF, D_MODEL), D_FF ** -0.5)
    return {'x_prompt': x_prompt, 'x_sample': x_sample,
            'state_ssd_conv': state_ssd_conv, 'state_ssd_ssm': state_ssd_ssm, 'state_hgrn': state_hgrn,
            'norm_mix_w': norm_mix_w, 'norm_mlp_w': norm_mlp_w, 'norm_f_w': norm_f_w,
            'ssd_w_in': ssd_w_in, 'ssd_conv_w': ssd_conv_w, 'ssd_conv_b': ssd_conv_b,
            'ssd_dt_bias': ssd_dt_bias, 'ssd_a_log': ssd_a_log, 'ssd_d': ssd_d,
            'ssd_norm_w': ssd_norm_w, 'ssd_w_out': ssd_w_out,
            'hgrn_w_in': hgrn_w_in, 'hgrn_lb_raw': hgrn_lb_raw, 'hgrn_norm_w': hgrn_norm_w, 'hgrn_w_out': hgrn_w_out,
            'mlp_w_up': mlp_w_up, 'mlp_w_down': mlp_w_down}


def reference(x_prompt, x_sample, state_ssd_conv, state_ssd_ssm, state_hgrn,
              norm_mix_w, norm_mlp_w, norm_f_w,
              ssd_w_in, ssd_conv_w, ssd_conv_b, ssd_dt_bias, ssd_a_log, ssd_d, ssd_norm_w, ssd_w_out,
              hgrn_w_in, hgrn_lb_raw, hgrn_norm_w, hgrn_w_out, mlp_w_up, mlp_w_down):
    p = jax.nn.softmax(hgrn_lb_raw.astype(jnp.float32), axis=0)
    hgrn_lb = jnp.cumsum(p, axis=0) - p[0]
    bp = x_prompt.shape[0]
    zero_conv = jnp.zeros((N_SSD_LAYERS, bp, SSD_CONV_W - 1, SSD_CONV_DIM), x_prompt.dtype)
    zero_ssm = jnp.zeros((N_SSD_LAYERS, bp, SSD_N_HEADS, SSD_HEAD_DIM, SSD_D_STATE), jnp.float32)
    zero_hgrn = jnp.zeros((N_HGRN_LAYERS, bp, HGRN_N_HEADS, HGRN_DK, HGRN_DV), jnp.float32)
    y_prompt, conv_p, ssm_p, hgrn_p = _trunk(
        x_prompt, zero_conv, zero_ssm, zero_hgrn, hgrn_lb, norm_mix_w, norm_mlp_w, norm_f_w,
        ssd_w_in, ssd_conv_w, ssd_conv_b, ssd_dt_bias, ssd_a_log, ssd_d, ssd_norm_w, ssd_w_out,
        hgrn_w_in, hgrn_norm_w, hgrn_w_out, mlp_w_up, mlp_w_down)
    y_sample, conv_s, ssm_s, hgrn_s = _trunk(
        x_sample, state_ssd_conv, state_ssd_ssm, state_hgrn, hgrn_lb, norm_mix_w, norm_mlp_w, norm_f_w,
        ssd_w_in, ssd_conv_w, ssd_conv_b, ssd_dt_bias, ssd_a_log, ssd_d, ssd_norm_w, ssd_w_out,
        hgrn_w_in, hgrn_norm_w, hgrn_w_out, mlp_w_up, mlp_w_down)
    return (y_prompt, y_sample, conv_p, ssm_p, hgrn_p, conv_s, ssm_s, hgrn_s)
```

```python
import functools

import jax
import jax.numpy as jnp
from jax import lax
from jax.experimental import pallas as pl
from jax.experimental.pallas import tpu as pltpu

F32 = jnp.float32
BF16 = jnp.bfloat16

D_MODEL = 1024
DEPTH = 4
SSD_D_INNER = 2048
SSD_HEAD_DIM = 64
SSD_N_HEADS = 32
SSD_N_GROUPS = 4
SSD_D_STATE = 128
SSD_CONV_W = 4
SSD_GN = SSD_N_GROUPS * SSD_D_STATE
SSD_CONV_DIM = SSD_D_INNER + 2 * SSD_GN
SSD_DT_PAD = 128
HGRN_N_HEADS = 8
HGRN_DK = 128
HGRN_DV = 128
HGRN_F = HGRN_N_HEADS * HGRN_DK
D_FF = 4 * D_MODEL
EPS = 1e-5
LB_FLOOR = 1e-20
NEG = -1e30
LANES = 128
SUBLANES = 8
VMEM_LIMIT = 56 * 1024 * 1024

SSD_PROMPT_T = 256
SSD_PROMPT_Q = 128
HGRN_PROMPT_T = 128
HGRN_PROMPT_C = 64
HGRN_SUB = 8
SAMPLE_BB = 8

_NT = (((1,), (1,)), ((), ()))
_TN = (((0,), (0,)), ((), ()))


def _sigmoid(x):
    return jax.nn.sigmoid(x)


def _silu(x):
    return x * jax.nn.sigmoid(x)


def _softplus(x):
    return jnp.maximum(x, 0.0) + jnp.log1p(jnp.exp(-jnp.abs(x)))


def _rms(x, w):
    ms = jnp.mean(x * x, axis=-1, keepdims=True)
    return x * lax.rsqrt(ms + EPS) * w


def _dot(a, b):
    return jnp.dot(a, b, preferred_element_type=F32)


def _dot_exact(a, b):
    return jnp.dot(a, b, precision=lax.Precision.HIGHEST, preferred_element_type=F32)


def _params(n_axes):
    return pltpu.CompilerParams(dimension_semantics=("arbitrary",) * n_axes,
                                vmem_limit_bytes=VMEM_LIMIT)


def _resident(shape):
    nd = len(shape)
    return pl.BlockSpec(shape, lambda *_: (0,) * nd, pipeline_mode=pl.Buffered(1))


def _norm_matmul_kernel(x_ref, nw_ref, w_ref, *o_refs):
    u = _rms(x_ref[...], nw_ref[...]).astype(BF16)
    c = 0
    for o_ref in o_refs:
        n = o_ref.shape[-1]
        o_ref[...] = _dot(u, w_ref[:, c:c + n])
        c += n


def _norm_matmul(x, nw, w, splits, tm, name):
    m, d = x.shape
    n_tot = w.shape[1]
    assert sum(splits) == n_tot and m % tm == 0
    return pl.pallas_call(
        _norm_matmul_kernel,
        grid=(m // tm,),
        in_specs=[pl.BlockSpec((tm, d), lambda i: (i, 0)),
                  _resident((1, d)),
                  _resident((d, n_tot))],
        out_specs=[pl.BlockSpec((tm, n), lambda i: (i, 0)) for n in splits],
        out_shape=[jax.ShapeDtypeStruct((m, n), F32) for n in splits],
        compiler_params=_params(1),
        name=name,
    )(x, nw, w)


def _gate_out_kernel(y_ref, g_ref, nw_ref, w_ref, h_ref, o_ref, *, n_groups, gate_first):
    y = y_ref[...]
    g = g_ref[...]
    sg = _silu(g)
    if gate_first:
        y = y * sg
    gs = y.shape[-1] // n_groups
    acc = h_ref[...]
    for k in range(n_groups):
        sl = slice(k * gs, (k + 1) * gs)
        yk = y[:, sl]
        ms = jnp.mean(yk * yk, axis=-1, keepdims=True)
        yn = yk * lax.rsqrt(ms + EPS) * nw_ref[:, sl]
        if not gate_first:
            yn = yn * sg[:, sl]
        acc = acc + _dot(yn.astype(BF16), w_ref[sl, :])
    o_ref[...] = acc


def _gate_out(y, g, nw, w, h, n_groups, gate_first, tm, name):
    m, di = y.shape
    d = h.shape[1]
    return pl.pallas_call(
        functools.partial(_gate_out_kernel, n_groups=n_groups, gate_first=gate_first),
        grid=(m // tm,),
        in_specs=[pl.BlockSpec((tm, di), lambda i: (i, 0)),
                  pl.BlockSpec((tm, di), lambda i: (i, 0)),
                  _resident((1, di)),
                  _resident((di, d)),
                  pl.BlockSpec((tm, d), lambda i: (i, 0))],
        out_specs=pl.BlockSpec((tm, d), lambda i: (i, 0)),
        out_shape=jax.ShapeDtypeStruct((m, d), F32),
        compiler_params=_params(1),
        name=name,
    )(y, g, nw, w, h)


def _mlp_kernel(x_ref, nw_ref, wu_ref, wd_ref, nf_ref, o_ref, *, final_norm, f_chunk):
    x = x_ref[...]
    u = _rms(x, nw_ref[...]).astype(BF16)
    acc = x
    for f in range(0, D_FF, f_chunk):
        mid = _dot(u, wu_ref[:, f:f + f_chunk])
        mid = jnp.square(jnp.maximum(mid, 0.0)).astype(BF16)
        acc = acc + _dot(mid, wd_ref[f:f + f_chunk, :])
    if final_norm:
        acc = _rms(acc, nf_ref[...])
    o_ref[...] = acc


def _mlp(x, nw, wu, wd, nf, final_norm, tm, name):
    m, d = x.shape
    return pl.pallas_call(
        functools.partial(_mlp_kernel, final_norm=final_norm, f_chunk=1024),
        grid=(m // tm,),
        in_specs=[pl.BlockSpec((tm, d), lambda i: (i, 0)),
                  _resident((1, d)),
                  _resident((d, D_FF)),
                  _resident((D_FF, d)),
                  _resident((1, d))],
        out_specs=pl.BlockSpec((tm, d), lambda i: (i, 0)),
        out_shape=jax.ShapeDtypeStruct((m, d), F32),
        compiler_params=_params(1),
        name=name,
    )(x, nw, wu, wd, nf)


def _ssd_prompt_kernel(xbc_ref, dt_ref, cw_ref, cb_ref, dtb_ref, a_ref, drep_ref,
                       y_ref, cst_ref, sst_ref, xext, xc, ht, *, t_len, q_len):
    i = pl.program_id(1)
    cdim = SSD_CONV_DIM

    @pl.when(i == 0)
    def _():
        xext[0:SUBLANES, :] = jnp.zeros((SUBLANES, cdim), F32)
        ht[...] = jnp.zeros_like(ht)

    xext[SUBLANES:SUBLANES + t_len, :] = xbc_ref[...]
    off = SUBLANES - (SSD_CONV_W - 1)
    for c0 in range(0, cdim, 512):
        cs = slice(c0, c0 + 512)
        acc = cb_ref[:, cs] + xext[off:off + t_len, cs] * cw_ref[0:1, cs]
        for w in range(1, SSD_CONV_W):
            acc = acc + xext[off + w:off + w + t_len, cs] * cw_ref[w:w + 1, cs]
        xc[:, cs] = _silu(acc)
    cst_ref[...] = xext[t_len + off:t_len + SUBLANES, :]
    xext[0:SUBLANES, :] = xext[t_len:t_len + SUBLANES, :]

    dt_all = _softplus(dt_ref[...] + dtb_ref[...])
    a_row = a_ref[...]
    ri = lax.broadcasted_iota(jnp.int32, (q_len, q_len), 0)
    ci = lax.broadcasted_iota(jnp.int32, (q_len, q_len), 1)
    causal = ri >= ci
    tril = causal.astype(F32)
    lane = lax.broadcasted_iota(jnp.int32, (1, LANES), 1)
    lo_mask = lane < SSD_HEAD_DIM

    for c in range(t_len // q_len):
        rs = slice(c * q_len, (c + 1) * q_len)
        dtc = dt_all[rs]
        cum = _dot_exact(tril, dtc * a_row)
        cum_t = cum.T
        dt_t = dtc.T
        for g in range(SSD_N_GROUPS):
            b_g = xc[rs, SSD_D_INNER + LANES * g:SSD_D_INNER + LANES * (g + 1)]
            c_g = xc[rs, SSD_D_INNER + SSD_GN + LANES * g:SSD_D_INNER + SSD_GN + LANES * (g + 1)]
            cbm = lax.dot_general(c_g.astype(BF16), b_g.astype(BF16), _NT, preferred_element_type=F32)
            b_gt = b_g.T
            for m in range(SSD_N_HEADS // SSD_N_GROUPS // 2):
                p = g * 4 + m
                ps = slice(LANES * p, LANES * (p + 1))
                xp = xc[rs, ps]
                h_old = ht[p]
                y_acc = xp * drep_ref[:, ps]
                h_new = None
                decs = []
                for s in range(2):
                    hh = 2 * p + s
                    msk = lo_mask if s == 0 else jnp.logical_not(lo_mask)
                    col = cum[:, hh:hh + 1]
                    row = cum_t[hh:hh + 1, :]
                    dt_row = dt_t[hh:hh + 1, :]
                    dec = jnp.exp(jnp.where(causal, col - row, NEG))
                    w_mat = (cbm * dec * dt_row).astype(BF16)
                    c_exp = (c_g * jnp.exp(col)).astype(BF16)
                    x_s = jnp.where(msk, xp, 0.0).astype(BF16)
                    h_s = jnp.where(msk, h_old, 0.0).astype(BF16)
                    y_acc = y_acc + _dot(w_mat, x_s) + _dot(c_exp, h_s)
                    last = cum[q_len - 1:q_len, hh:hh + 1]
                    b_s = (b_gt * (jnp.exp(last - row) * dt_row)).astype(BF16)
                    upd = _dot(b_s, x_s)
                    h_new = upd if h_new is None else h_new + upd
                    decs.append(jnp.exp(last))
                dec_row = jnp.where(lo_mask, decs[0], decs[1])
                ht[p] = h_old * dec_row + h_new
                y_ref[rs, ps] = y_acc

    @pl.when(i == pl.num_programs(1) - 1)
    def _():
        for p in range(SSD_N_HEADS // 2):
            sst_ref[LANES * p:LANES * (p + 1), :] = ht[p].T


def _ssd_prompt_scan(xbc, dt_raw, cw, cb, dtb, a_neg, d_rep, batch, seq):
    t_len, q_len = SSD_PROMPT_T, SSD_PROMPT_Q
    nt = seq // t_len
    tok = lambda b, i: (b * nt + i, 0)
    return pl.pallas_call(
        functools.partial(_ssd_prompt_kernel, t_len=t_len, q_len=q_len),
        grid=(batch, nt),
        in_specs=[pl.BlockSpec((t_len, SSD_CONV_DIM), tok),
                  pl.BlockSpec((t_len, SSD_DT_PAD), tok),
                  _resident((SSD_CONV_W, SSD_CONV_DIM)),
                  _resident((1, SSD_CONV_DIM)),
                  _resident((1, SSD_DT_PAD)),
                  _resident((1, SSD_DT_PAD)),
                  _resident((1, SSD_D_INNER))],
        out_specs=[pl.BlockSpec((t_len, SSD_D_INNER), tok),
                   pl.BlockSpec((None, SSD_CONV_W - 1, SSD_CONV_DIM), lambda b, i: (b, 0, 0)),
                   pl.BlockSpec((None, SSD_N_HEADS * SSD_HEAD_DIM, SSD_D_STATE), lambda b, i: (b, 0, 0))],
        out_shape=[jax.ShapeDtypeStruct((batch * seq, SSD_D_INNER), F32),
                   jax.ShapeDtypeStruct((batch, SSD_CONV_W - 1, SSD_CONV_DIM), F32),
                   jax.ShapeDtypeStruct((batch, SSD_N_HEADS * SSD_HEAD_DIM, SSD_D_STATE), F32)],
        scratch_shapes=[pltpu.VMEM((t_len + SUBLANES, SSD_CONV_DIM), F32),
                        pltpu.VMEM((t_len, SSD_CONV_DIM), F32),
                        pltpu.VMEM((SSD_N_HEADS // 2, SSD_D_STATE, LANES), F32)],
        compiler_params=_params(2),
        name="ssd_prompt_scan",
    )(xbc, dt_raw, cw, cb, dtb, a_neg, d_rep)


def _hgrn_gates(q_raw, fz, lb, loglb):
    l1p = jnp.log1p(jnp.exp(-jnp.abs(fz)))
    ls_pos = jnp.minimum(fz, 0.0) - l1p
    ls_neg = jnp.minimum(-fz, 0.0) - l1p
    b = loglb + ls_neg
    logf = jnp.maximum(ls_pos, b) + jnp.log1p(jnp.exp(-jnp.abs(ls_pos - b)))
    k = (1.0 - lb) * _sigmoid(-fz)
    return _silu(q_raw), k, logf


def _hgrn_prompt_kernel(q_ref, f_ref, v_ref, lb_ref, loglb_ref, o_ref, st_ref,
                        qs, ks, cs, stt, *, t_len, c_len):
    i = pl.program_id(1)
    nh = HGRN_N_HEADS
    sub = HGRN_SUB

    @pl.when(i == 0)
    def _():
        stt[...] = jnp.zeros_like(stt)

    ri = lax.broadcasted_iota(jnp.int32, (c_len, c_len), 0)
    ci = lax.broadcasted_iota(jnp.int32, (c_len, c_len), 1)
    tril = (ri >= ci).astype(F32)
    for h in range(nh):
        hs = slice(LANES * h, LANES * (h + 1))
        q, k, logf = _hgrn_gates(q_ref[:, hs], f_ref[:, hs], lb_ref[:, hs], loglb_ref[:, hs])
        qs[:, hs] = q
        ks[:, hs] = k
        for c in range(t_len // c_len):
            rs = slice(c * c_len, (c + 1) * c_len)
            cs[rs, hs] = _dot_exact(tril, logf[rs])

    row_c = lax.broadcasted_iota(jnp.int32, (c_len, LANES), 0)
    row_t = lax.broadcasted_iota(jnp.int32, (t_len, LANES), 0)
    row_in_sub = row_t % sub
    levels = []
    s = c_len // 2
    while s >= sub:
        levels.append(s)
        s //= 2

    for h in range(nh):
        hs = slice(LANES * h, LANES * (h + 1))
        q_all = qs[:, hs]
        cum_all = cs[:, hs]
        o_diag = jnp.zeros((t_len, LANES), F32)
        for j in range(sub):
            def rows(ref):
                return jnp.concatenate(
                    [jnp.broadcast_to(ref[sub * a + j:sub * a + j + 1, hs], (sub, LANES))
                     for a in range(t_len // sub)], axis=0)
            k_b, cum_b, v_b = rows(ks), rows(cs), rows(v_ref)
            arg = jnp.where(row_in_sub >= j, cum_all - cum_b, NEG)
            z = q_all * k_b * jnp.exp(arg)
            o_diag = o_diag + jnp.sum(z, axis=-1, keepdims=True) * v_b

        for c in range(t_len // c_len):
            rs = slice(c * c_len, (c + 1) * c_len)
            qh, kh, vh, cum = qs[rs, hs], ks[rs, hs], v_ref[rs, hs], cs[rs, hs]
            st_old = stt[h]
            vb16 = vh.astype(BF16)
            o = lax.dot_general((qh * jnp.exp(cum)).astype(BF16), st_old.astype(BF16), _NT,
                                preferred_element_type=F32)
            att = jnp.zeros((c_len, c_len), F32)
            for s in levels:
                ref_rows = jnp.concatenate(
                    [jnp.broadcast_to(cum[blk * 2 * s + s - 1:blk * 2 * s + s, :], (2 * s, LANES))
                     for blk in range(c_len // (2 * s))], axis=0)
                second = (row_c // s) % 2 == 1
                e = jnp.exp(jnp.where(second, cum - ref_rows, ref_rows - cum))
                q_t = jnp.where(second, qh * e, 0.0).astype(BF16)
                k_t = jnp.where(second, 0.0, kh * e).astype(BF16)
                part = lax.dot_general(q_t, k_t, _NT, preferred_element_type=F32)
                if 2 * s < c_len:
                    part = jnp.where((ri // (2 * s)) == (ci // (2 * s)), part, 0.0)
                att = att + part
            o = o + _dot(att.astype(BF16), vb16)
            o_ref[rs, hs] = o + o_diag[rs]
            last = cum[c_len - 1:c_len, :]
            k_hat = (kh * jnp.exp(last - cum)).astype(BF16)
            upd = lax.dot_general(vb16, k_hat, _TN, preferred_element_type=F32)
            stt[h] = st_old * jnp.exp(last) + upd

    @pl.when(i == pl.num_programs(1) - 1)
    def _():
        for h in range(nh):
            st_ref[h] = stt[h].T


def _hgrn_prompt_scan(qr, fz, v, lb, loglb, batch, seq):
    t_len, c_len = HGRN_PROMPT_T, HGRN_PROMPT_C
    nt = seq // t_len
    tok = lambda b, i: (b * nt + i, 0)
    return pl.pallas_call(
        functools.partial(_hgrn_prompt_kernel, t_len=t_len, c_len=c_len),
        grid=(batch, nt),
        in_specs=[pl.BlockSpec((t_len, HGRN_F), tok),
                  pl.BlockSpec((t_len, HGRN_F), tok),
                  pl.BlockSpec((t_len, D_MODEL), tok),
                  _resident((1, HGRN_F)),
                  _resident((1, HGRN_F))],
        out_specs=[pl.BlockSpec((t_len, D_MODEL), tok),
                   pl.BlockSpec((None, HGRN_N_HEADS, HGRN_DK, HGRN_DV), lambda b, i: (b, 0, 0, 0))],
        out_shape=[jax.ShapeDtypeStruct((batch * seq, D_MODEL), F32),
                   jax.ShapeDtypeStruct((batch, HGRN_N_HEADS, HGRN_DK, HGRN_DV), F32)],
        scratch_shapes=[pltpu.VMEM((t_len, HGRN_F), F32),
                        pltpu.VMEM((t_len, HGRN_F), F32),
                        pltpu.VMEM((t_len, HGRN_F), F32),
                        pltpu.VMEM((HGRN_N_HEADS, HGRN_DV, HGRN_DK), F32)],
        compiler_params=_params(2),
        name="hgrn_prompt_scan",
    )(qr, fz, v, lb, loglb)


def _ssd_sample_kernel(xbc_ref, dt_ref, cst_ref, sst_ref, cw_ref, cb_ref, dtb_ref, a_ref, drep_ref,
                       e16_ref, e32_ref, y_ref, ncst_ref, nsst_ref, xc, xh, yi, *, n_tok, bb):
    nw = SSD_CONV_W
    ext = [cst_ref[s] for s in range(nw - 1)] + [xbc_ref[t] for t in range(n_tok)]
    for t in range(n_tok):
        acc = cb_ref[...] + ext[t] * cw_ref[0:1, :]
        for w in range(1, nw):
            acc = acc + ext[t + w] * cw_ref[w:w + 1, :]
        xc[t] = _silu(acc)
    for s in range(nw - 1):
        ncst_ref[s] = ext[n_tok + s]

    a_row = a_ref[...]
    dts, cums = [], []
    run = None
    for t in range(n_tok):
        dt = _softplus(dt_ref[t] + dtb_ref[...])
        run = dt * a_row if run is None else run + dt * a_row
        dts.append(dt)
        cums.append(run)

    lane = lax.broadcasted_iota(jnp.int32, (bb, LANES), 1)
    rpg = SSD_N_HEADS // SSD_N_GROUPS

    def group_dot(t, j):
        out = None
        for g in reversed(range(SSD_N_GROUPS)):
            cg = xc[t, :, SSD_D_INNER + SSD_GN + LANES * g:SSD_D_INNER + SSD_GN + LANES * (g + 1)]
            bg = xc[j, :, SSD_D_INNER + LANES * g:SSD_D_INNER + LANES * (g + 1)]
            sg = jnp.sum(cg * bg, axis=-1, keepdims=True)
            out = jnp.broadcast_to(sg, (bb, LANES)) if out is None else jnp.where(lane < rpg * (g + 1), sg, out)
        return out

    pairs = [(t, j) for t in range(n_tok) for j in range(t + 1)]
    w_rows = [group_dot(t, j) * jnp.exp(cums[t] - cums[j]) * dts[j] for (t, j) in pairs]
    tail_rows = [jnp.exp(cums[n_tok - 1] - cums[j]) * dts[j] for j in range(n_tok)]
    we = _dot(jnp.concatenate(w_rows + tail_rows, axis=0).astype(BF16), e16_ref[...])
    ce = _dot_exact(jnp.concatenate([jnp.exp(cums[t]) for t in range(n_tok)], axis=0), e32_ref[...])

    for j in range(n_tok):
        r0 = (len(pairs) + j) * bb
        xh[j] = we[r0:r0 + bb] * xc[j, :, 0:SSD_D_INNER]

    dec = jnp.exp(cums[n_tok - 1])
    gw = SSD_D_INNER // SSD_N_GROUPS
    for b in range(bb):
        for g in range(SSD_N_GROUPS):
            cb_ = xc[:, b, SSD_D_INNER + SSD_GN + LANES * g:SSD_D_INNER + SSD_GN + LANES * (g + 1)]
            bb_ = xc[:, b, SSD_D_INNER + LANES * g:SSD_D_INNER + LANES * (g + 1)]
            h0 = sst_ref[b, gw * g:gw * (g + 1), :]
            yi[:, b, gw * g:gw * (g + 1)] = lax.dot_general(
                cb_.astype(BF16), h0.astype(BF16), _NT, preferred_element_type=F32)
            xhb = xh[:, b, gw * g:gw * (g + 1)]
            upd = lax.dot_general(xhb.astype(BF16), bb_.astype(BF16), _TN, preferred_element_type=F32)
            for r in range(rpg):
                hh = rpg * g + r
                d = dec[b:b + 1, hh:hh + 1]
                rows = slice(SSD_HEAD_DIM * r, SSD_HEAD_DIM * (r + 1))
                nsst_ref[b, gw * g + SSD_HEAD_DIM * r:gw * g + SSD_HEAD_DIM * (r + 1), :] = h0[rows] * d + upd[rows]

    for t in range(n_tok):
        xt = xc[t, :, 0:SSD_D_INNER]
        y = xt * drep_ref[...] + yi[t] * ce[t * bb:(t + 1) * bb]
        for j in range(t + 1):
            r0 = pairs.index((t, j)) * bb
            y = y + we[r0:r0 + bb] * xc[j, :, 0:SSD_D_INNER]
        y_ref[t] = y


def _ssd_sample_scan(xbc, dt_raw, cst, sst, cw, cb, dtb, a_neg, d_rep, e16, e32, n_tok, batch):
    bb = SAMPLE_BB
    rows = SSD_N_HEADS * SSD_HEAD_DIM
    tb = lambda i: (0, i, 0)
    return pl.pallas_call(
        functools.partial(_ssd_sample_kernel, n_tok=n_tok, bb=bb),
        grid=(batch // bb,),
        in_specs=[pl.BlockSpec((n_tok, bb, SSD_CONV_DIM), tb),
                  pl.BlockSpec((n_tok, bb, SSD_DT_PAD), tb),
                  pl.BlockSpec((SSD_CONV_W - 1, bb, SSD_CONV_DIM), tb),
                  pl.BlockSpec((bb, rows, SSD_D_STATE), lambda i: (i, 0, 0)),
                  _resident((SSD_CONV_W, SSD_CONV_DIM)),
                  _resident((1, SSD_CONV_DIM)),
                  _resident((1, SSD_DT_PAD)),
                  _resident((1, SSD_DT_PAD)),
                  _resident((1, SSD_D_INNER)),
                  _resident((LANES, SSD_D_INNER)),
                  _resident((LANES, SSD_D_INNER))],
        out_specs=[pl.BlockSpec((n_tok, bb, SSD_D_INNER), tb),
                   pl.BlockSpec((SSD_CONV_W - 1, bb, SSD_CONV_DIM), tb),
                   pl.BlockSpec((bb, rows, SSD_D_STATE), lambda i: (i, 0, 0))],
        out_shape=[jax.ShapeDtypeStruct((n_tok, batch, SSD_D_INNER), F32),
                   jax.ShapeDtypeStruct((SSD_CONV_W - 1, batch, SSD_CONV_DIM), F32),
                   jax.ShapeDtypeStruct((batch, rows, SSD_D_STATE), F32)],
        scratch_shapes=[pltpu.VMEM((n_tok, bb, SSD_CONV_DIM), F32),
                        pltpu.VMEM((n_tok, bb, SSD_D_INNER), F32),
                        pltpu.VMEM((n_tok, bb, SSD_D_INNER), F32)],
        compiler_params=_params(1),
        name="ssd_sample_scan",
    )(xbc, dt_raw, cst, sst, cw, cb, dtb, a_neg, d_rep, e16, e32)


def _hgrn_sample_kernel(q_ref, f_ref, v_ref, st_ref, lb_ref, loglb_ref, o_ref, nst_ref,
                        qe, kh, oi, *, n_tok, bb):
    nh = HGRN_N_HEADS
    qs, ks, cums = [], [], []
    run = None
    for t in range(n_tok):
        q, k, logf = _hgrn_gates(q_ref[t], f_ref[t], lb_ref[...], loglb_ref[...])
        run = logf if run is None else run + logf
        qs.append(q)
        ks.append(k)
        cums.append(run)
    last = cums[n_tok - 1]
    for t in range(n_tok):
        qe[t] = qs[t] * jnp.exp(cums[t])
        kh[t] = ks[t] * jnp.exp(last - cums[t])
    dec = jnp.exp(last)

    for h in range(nh):
        hs = slice(LANES * h, LANES * (h + 1))
        dec_t = dec[:, hs].T
        for b in range(bb):
            s0 = st_ref[b, h]
            oi[:, b, hs] = _dot(qe[:, b, hs].astype(BF16), s0.astype(BF16))
            upd = lax.dot_general(kh[:, b, hs].astype(BF16), v_ref[:, b, hs].astype(BF16), _TN,
                                  preferred_element_type=F32)
            nst_ref[b, h] = s0 * dec_t[:, b:b + 1] + upd

    for t in range(n_tok):
        o = oi[t]
        for j in range(t + 1):
            z = qs[t] * ks[j] * jnp.exp(cums[t] - cums[j])
            vj = v_ref[j]
            parts = []
            for h in range(nh):
                hs = slice(LANES * h, LANES * (h + 1))
                parts.append(jnp.sum(z[:, hs], axis=-1, keepdims=True) * vj[:, hs])
            o = o + jnp.concatenate(parts, axis=-1)
        o_ref[t] = o


def _hgrn_sample_scan(qr, fz, v, st, lb, loglb, n_tok, batch):
    bb = SAMPLE_BB
    tb = lambda i: (0, i, 0)
    st_spec = pl.BlockSpec((bb, HGRN_N_HEADS, HGRN_DK, HGRN_DV), lambda i: (i, 0, 0, 0))
    return pl.pallas_call(
        functools.partial(_hgrn_sample_kernel, n_tok=n_tok, bb=bb),
        grid=(batch // bb,),
        in_specs=[pl.BlockSpec((n_tok, bb, HGRN_F), tb),
                  pl.BlockSpec((n_tok, bb, HGRN_F), tb),
                  pl.BlockSpec((n_tok, bb, D_MODEL), tb),
                  st_spec,
                  _resident((1, HGRN_F)),
                  _resident((1, HGRN_F))],
        out_specs=[pl.BlockSpec((n_tok, bb, D_MODEL), tb), st_spec],
        out_shape=[jax.ShapeDtypeStruct((n_tok, batch, D_MODEL), F32),
                   jax.ShapeDtypeStruct((batch, HGRN_N_HEADS, HGRN_DK, HGRN_DV), F32)],
        scratch_shapes=[pltpu.VMEM((n_tok, bb, HGRN_F), F32),
                        pltpu.VMEM((n_tok, bb, HGRN_F), F32),
                        pltpu.VMEM((n_tok, bb, D_MODEL), F32)],
        compiler_params=_params(1),
        name="hgrn_sample_scan",
    )(qr, fz, v, st, lb, loglb)


def _prep_params(norm_mix_w, norm_mlp_w, norm_f_w, ssd_w_in, ssd_conv_w, ssd_conv_b, ssd_dt_bias,
                 ssd_a_log, ssd_d, ssd_norm_w, ssd_w_out, hgrn_w_in, hgrn_lb_raw, hgrn_norm_w,
                 hgrn_w_out, mlp_w_up, mlp_w_down):
    la = ssd_w_in.shape[0]
    pad_dt = SSD_DT_PAD - SSD_N_HEADS
    p = {}
    p["norm_mix"] = norm_mix_w.reshape(DEPTH, 1, D_MODEL)
    p["norm_mlp"] = norm_mlp_w.reshape(DEPTH, 1, D_MODEL)
    p["norm_f"] = norm_f_w.reshape(1, D_MODEL)
    p["ssd_w_in"] = jnp.pad(ssd_w_in, ((0, 0), (0, 0), (0, pad_dt))).astype(BF16)
    p["ssd_conv_w"] = ssd_conv_w
    p["ssd_conv_b"] = ssd_conv_b.reshape(la, 1, SSD_CONV_DIM)
    p["ssd_dt_bias"] = jnp.pad(ssd_dt_bias, ((0, 0), (0, pad_dt))).reshape(la, 1, SSD_DT_PAD)
    p["ssd_a"] = jnp.pad(-jnp.exp(ssd_a_log.astype(F32)), ((0, 0), (0, pad_dt))).reshape(la, 1, SSD_DT_PAD)
    p["ssd_d_rep"] = jnp.repeat(ssd_d.astype(F32), SSD_HEAD_DIM, axis=1).reshape(la, 1, SSD_D_INNER)
    p["ssd_norm"] = ssd_norm_w.reshape(la, 1, SSD_D_INNER)
    p["ssd_w_out"] = ssd_w_out.astype(BF16)
    p["hgrn_w_in"] = hgrn_w_in.astype(BF16)
    sm = jax.nn.softmax(hgrn_lb_raw.astype(F32), axis=0)
    lb = jnp.cumsum(sm, axis=0) - sm[0]
    lbn = lb.shape[0]
    p["hgrn_lb"] = lb.reshape(lbn, 1, HGRN_F)
    p["hgrn_loglb"] = jnp.log(jnp.maximum(lb, LB_FLOOR)).reshape(lbn, 1, HGRN_F)
    p["hgrn_norm"] = hgrn_norm_w.reshape(lbn, 1, D_MODEL)
    p["hgrn_w_out"] = hgrn_w_out.astype(BF16)
    p["mlp_w_up"] = mlp_w_up.astype(BF16)
    p["mlp_w_down"] = mlp_w_down.astype(BF16)
    head_of_lane = jnp.arange(SSD_D_INNER, dtype=jnp.int32) // SSD_HEAD_DIM
    expand = (jnp.arange(LANES, dtype=jnp.int32)[:, None] == head_of_lane[None, :])
    p["expand16"] = expand.astype(BF16)
    p["expand32"] = expand.astype(F32)
    return p


def _trunk(h, p, prompt, batch, seq, conv_states, ssm_states, hgrn_states):
    tm = min(256, h.shape[0])
    new_conv, new_ssm, new_hgrn = [], [], []
    for layer in range(DEPTH):
        j = layer // 2
        if layer % 2 == 0:
            z, xbc, dt_raw = _norm_matmul(h, p["norm_mix"][layer], p["ssd_w_in"][j],
                                          (SSD_D_INNER, SSD_CONV_DIM, SSD_DT_PAD), tm, "ssd_in_proj")
            small = (p["ssd_conv_w"][j], p["ssd_conv_b"][j], p["ssd_dt_bias"][j], p["ssd_a"][j], p["ssd_d_rep"][j])
            if prompt:
                y, cst, sst = _ssd_prompt_scan(xbc, dt_raw, *small, batch, seq)
                sst = sst.reshape(batch, SSD_N_HEADS, SSD_HEAD_DIM, SSD_D_STATE)
            else:
                y, cst, sst = _ssd_sample_scan(
                    xbc.reshape(seq, batch, SSD_CONV_DIM), dt_raw.reshape(seq, batch, SSD_DT_PAD),
                    jnp.swapaxes(conv_states[j], 0, 1),
                    ssm_states[j].reshape(batch, SSD_N_HEADS * SSD_HEAD_DIM, SSD_D_STATE),
                    *small, p["expand16"], p["expand32"], seq, batch)
                y = y.reshape(seq * batch, SSD_D_INNER)
                cst = jnp.swapaxes(cst, 0, 1)
                sst = sst.reshape(batch, SSD_N_HEADS, SSD_HEAD_DIM, SSD_D_STATE)
            new_conv.append(cst)
            new_ssm.append(sst)
            h = _gate_out(y, z, p["ssd_norm"][j], p["ssd_w_out"][j], h, SSD_N_GROUPS, True, tm, "ssd_out_proj")
        else:
            qr, fz, v, g = _norm_matmul(h, p["norm_mix"][layer], p["hgrn_w_in"][j],
                                        (HGRN_F, HGRN_F, D_MODEL, D_MODEL), tm, "hgrn_in_proj")
            if prompt:
                o, st = _hgrn_prompt_scan(qr, fz, v, p["hgrn_lb"][j], p["hgrn_loglb"][j], batch, seq)
            else:
                o, st = _hgrn_sample_scan(qr.reshape(seq, batch, HGRN_F), fz.reshape(seq, batch, HGRN_F),
                                          v.reshape(seq, batch, D_MODEL), hgrn_states[j],
                                          p["hgrn_lb"][j], p["hgrn_loglb"][j], seq, batch)
                o = o.reshape(seq * batch, D_MODEL)
            new_hgrn.append(st)
            h = _gate_out(o, g, p["hgrn_norm"][j], p["hgrn_w_out"][j], h, HGRN_N_HEADS, False, tm, "hgrn_out_proj")
        h = _mlp(h, p["norm_mlp"][layer], p["mlp_w_up"][layer], p["mlp_w_down"][layer], p["norm_f"],
                 layer == DEPTH - 1, tm, "mlp")
    return h, jnp.stack(new_conv), jnp.stack(new_ssm), jnp.stack(new_hgrn)


def kernel(x_prompt, x_sample, state_ssd_conv, state_ssd_ssm, state_hgrn, norm_mix_w, norm_mlp_w, norm_f_w,
           ssd_w_in, ssd_conv_w, ssd_conv_b, ssd_dt_bias, ssd_a_log, ssd_d, ssd_norm_w, ssd_w_out,
           hgrn_w_in, hgrn_lb_raw, hgrn_norm_w, hgrn_w_out, mlp_w_up, mlp_w_down):
    p = _prep_params(norm_mix_w, norm_mlp_w, norm_f_w, ssd_w_in, ssd_conv_w, ssd_conv_b, ssd_dt_bias,
                     ssd_a_log, ssd_d, ssd_norm_w, ssd_w_out, hgrn_w_in, hgrn_lb_raw, hgrn_norm_w,
                     hgrn_w_out, mlp_w_up, mlp_w_down)
    bp, lp, d = x_prompt.shape
    bs, ls, _ = x_sample.shape
    y_p, conv_p, ssm_p, hgrn_p = _trunk(x_prompt.reshape(bp * lp, d), p, True, bp, lp, None, None, None)
    xs = jnp.swapaxes(x_sample, 0, 1).reshape(ls * bs, d)
    y_s, conv_s, ssm_s, hgrn_s = _trunk(xs, p, False, bs, ls, state_ssd_conv, state_ssd_ssm, state_hgrn)
    y_s = jnp.swapaxes(y_s.reshape(ls, bs, d), 0, 1)
    return (y_p.reshape(bp, lp, d), y_s, conv_p, ssm_p, hgrn_p, conv_s, ssm_s, hgrn_s)
```

```python
import functools

import jax
import jax.numpy as jnp
from jax import lax
from jax.experimental import pallas as pl
from jax.experimental.pallas import tpu as pltpu

F32 = jnp.float32
BF16 = jnp.bfloat16

D_MODEL = 1024
DEPTH = 4
SSD_D_INNER = 2048
SSD_HEAD_DIM = 64
SSD_N_HEADS = 32
SSD_N_GROUPS = 4
SSD_D_STATE = 128
SSD_CONV_W = 4
SSD_GN = SSD_N_GROUPS * SSD_D_STATE
SSD_CONV_DIM = SSD_D_INNER + 2 * SSD_GN
SSD_DT_PAD = 128
HGRN_N_HEADS = 8
HGRN_DK = 128
HGRN_DV = 128
HGRN_F = HGRN_N_HEADS * HGRN_DK
D_FF = 4 * D_MODEL
EPS = 1e-5
LB_FLOOR = 1e-20
NEG = -1e30
LANES = 128
SUBLANES = 8
VMEM_LIMIT = 56 * 1024 * 1024

TOKEN_TILE = 512
SSD_PROMPT_T = 256
SSD_PROMPT_Q = 128
HGRN_PROMPT_T = 128
HGRN_PROMPT_C = 64
HGRN_SUB = 8
HGRN_FAST_SUB = 32
HGRN_FAST_RANGE = 60.0
SAMPLE_BB = 8

_NT = (((1,), (1,)), ((), ()))
_TN = (((0,), (0,)), ((), ()))


def _sigmoid(x):
    return jax.nn.sigmoid(x)


def _silu(x):
    return x * jax.nn.sigmoid(x)


def _softplus(x):
    return jnp.maximum(x, 0.0) + jnp.log1p(jnp.exp(-jnp.abs(x)))


def _rms(x, w):
    ms = jnp.mean(x * x, axis=-1, keepdims=True)
    return x * lax.rsqrt(ms + EPS) * w


def _dot(a, b):
    return jnp.dot(a, b, preferred_element_type=F32)


def _dot_exact(a, b):
    return jnp.dot(a, b, precision=lax.Precision.HIGHEST, preferred_element_type=F32)


def _cumsum_rows(tril, x):
    hi = x.astype(BF16)
    r1 = x - hi.astype(F32)
    mid = r1.astype(BF16)
    lo = (r1 - mid.astype(F32)).astype(BF16)
    return _dot(tril, hi) + _dot(tril, mid) + _dot(tril, lo)


def _params(n_axes):
    return pltpu.CompilerParams(dimension_semantics=("arbitrary",) * n_axes,
                                vmem_limit_bytes=VMEM_LIMIT)


def _resident(shape):
    nd = len(shape)
    return pl.BlockSpec(shape, lambda *_: (0,) * nd, pipeline_mode=pl.Buffered(1))


def _layer_resident(shape, layer):
    nd = len(shape)
    return pl.BlockSpec((None,) + tuple(shape), lambda *_: (layer,) + (0,) * nd,
                        pipeline_mode=pl.Buffered(1))


def _norm_matmul_kernel(x_ref, nw_ref, w_ref, *o_refs):
    u = _rms(x_ref[...], nw_ref[...]).astype(BF16)
    c = 0
    for o_ref in o_refs:
        n = o_ref.shape[-1]
        o_ref[...] = _dot(u, w_ref[:, c:c + n])
        c += n


def _norm_matmul(x, nw_all, nw_layer, w_all, w_layer, splits, tm, name):
    m, d = x.shape
    n_tot = w_all.shape[-1]
    assert sum(splits) == n_tot and m % tm == 0
    return pl.pallas_call(
        _norm_matmul_kernel,
        grid=(m // tm,),
        in_specs=[pl.BlockSpec((tm, d), lambda i: (i, 0)),
                  _layer_resident((1, d), nw_layer),
                  _layer_resident((d, n_tot), w_layer)],
        out_specs=[pl.BlockSpec((tm, n), lambda i: (i, 0)) for n in splits],
        out_shape=[jax.ShapeDtypeStruct((m, n), F32) for n in splits],
        compiler_params=_params(1),
        name=name,
    )(x, nw_all, w_all)


def _gate_out_kernel(y_ref, g_ref, nw_ref, w_ref, h_ref, o_ref, *, n_groups, gate_first):
    y = y_ref[...]
    g = g_ref[...]
    sg = _silu(g)
    if gate_first:
        y = y * sg
    gs = y.shape[-1] // n_groups
    acc = h_ref[...]
    for k in range(n_groups):
        sl = slice(k * gs, (k + 1) * gs)
        yk = y[:, sl]
        ms = jnp.mean(yk * yk, axis=-1, keepdims=True)
        yn = yk * lax.rsqrt(ms + EPS) * nw_ref[:, sl]
        if not gate_first:
            yn = yn * sg[:, sl]
        acc = acc + _dot(yn.astype(BF16), w_ref[sl, :])
    o_ref[...] = acc


def _gate_out(y, g, nw_all, w_all, layer, h, n_groups, gate_first, tm, name):
    m, di = y.shape
    d = h.shape[1]
    return pl.pallas_call(
        functools.partial(_gate_out_kernel, n_groups=n_groups, gate_first=gate_first),
        grid=(m // tm,),
        in_specs=[pl.BlockSpec((tm, di), lambda i: (i, 0)),
                  pl.BlockSpec((tm, di), lambda i: (i, 0)),
                  _layer_resident((1, di), layer),
                  _layer_resident((di, d), layer),
                  pl.BlockSpec((tm, d), lambda i: (i, 0))],
        out_specs=pl.BlockSpec((tm, d), lambda i: (i, 0)),
        out_shape=jax.ShapeDtypeStruct((m, d), F32),
        compiler_params=_params(1),
        name=name,
    )(y, g, nw_all, w_all, h)


def _mlp_kernel(x_ref, nw_ref, wu_ref, wd_ref, nf_ref, o_ref, *, final_norm, f_chunk):
    x = x_ref[...]
    u = _rms(x, nw_ref[...]).astype(BF16)
    acc = x
    for f in range(0, D_FF, f_chunk):
        mid = _dot(u, wu_ref[:, f:f + f_chunk])
        mid = jnp.square(jnp.maximum(mid, 0.0)).astype(BF16)
        acc = acc + _dot(mid, wd_ref[f:f + f_chunk, :])
    if final_norm:
        acc = _rms(acc, nf_ref[...])
    o_ref[...] = acc


def _mlp(x, nw_all, wu_all, wd_all, layer, nf, final_norm, tm, name):
    m, d = x.shape
    return pl.pallas_call(
        functools.partial(_mlp_kernel, final_norm=final_norm, f_chunk=1024),
        grid=(m // tm,),
        in_specs=[pl.BlockSpec((tm, d), lambda i: (i, 0)),
                  _layer_resident((1, d), layer),
                  _layer_resident((d, D_FF), layer),
                  _layer_resident((D_FF, d), layer),
                  _resident((1, d))],
        out_specs=pl.BlockSpec((tm, d), lambda i: (i, 0)),
        out_shape=jax.ShapeDtypeStruct((m, d), F32),
        compiler_params=_params(1),
        name=name,
    )(x, nw_all, wu_all, wd_all, nf)


def _ssd_small_specs(layer):
    return [_layer_resident((SSD_CONV_W, SSD_CONV_DIM), layer),
            _layer_resident((1, SSD_CONV_DIM), layer),
            _layer_resident((1, SSD_DT_PAD), layer),
            _layer_resident((1, SSD_DT_PAD), layer),
            _layer_resident((1, SSD_D_INNER), layer)]


def _ssd_prompt_kernel(xbc_ref, dt_ref, cw_ref, cb_ref, dtb_ref, a_ref, drep_ref,
                       y_ref, cst_ref, sst_ref, xext, xc, ht, *, t_len, q_len):
    i = pl.program_id(1)
    cdim = SSD_CONV_DIM

    @pl.when(i == 0)
    def _():
        xext[0:SUBLANES, :] = jnp.zeros((SUBLANES, cdim), F32)
        ht[...] = jnp.zeros_like(ht)

    xext[SUBLANES:SUBLANES + t_len, :] = xbc_ref[...]
    off = SUBLANES - (SSD_CONV_W - 1)
    for c0 in range(0, cdim, 512):
        cs = slice(c0, c0 + 512)
        acc = cb_ref[:, cs] + xext[off:off + t_len, cs] * cw_ref[0:1, cs]
        for w in range(1, SSD_CONV_W):
            acc = acc + xext[off + w:off + w + t_len, cs] * cw_ref[w:w + 1, cs]
        xc[:, cs] = _silu(acc)
    cst_ref[...] = xext[t_len + off:t_len + SUBLANES, :]
    xext[0:SUBLANES, :] = xext[t_len:t_len + SUBLANES, :]

    dt_all = _softplus(dt_ref[...] + dtb_ref[...])
    a_row = a_ref[...]
    ri = lax.broadcasted_iota(jnp.int32, (q_len, q_len), 0)
    ci = lax.broadcasted_iota(jnp.int32, (q_len, q_len), 1)
    causal = ri >= ci
    tril = jnp.where(causal, 1.0, 0.0).astype(BF16)
    lane = lax.broadcasted_iota(jnp.int32, (1, LANES), 1)
    lo_mask = lane < SSD_HEAD_DIM

    for c in range(t_len // q_len):
        rs = slice(c * q_len, (c + 1) * q_len)
        dtc = dt_all[rs]
        cum = _cumsum_rows(tril, dtc * a_row)
        cum_t = cum.T
        dt_t = dtc.T
        for g in range(SSD_N_GROUPS):
            b_g = xc[rs, SSD_D_INNER + LANES * g:SSD_D_INNER + LANES * (g + 1)]
            c_g = xc[rs, SSD_D_INNER + SSD_GN + LANES * g:SSD_D_INNER + SSD_GN + LANES * (g + 1)]
            cbm = lax.dot_general(c_g.astype(BF16), b_g.astype(BF16), _NT, preferred_element_type=F32)
            b_gt = b_g.T
            for m in range(SSD_N_HEADS // SSD_N_GROUPS // 2):
                p = g * 4 + m
                ps = slice(LANES * p, LANES * (p + 1))
                xp = xc[rs, ps]
                h_old = ht[p]
                y_acc = xp * drep_ref[:, ps]
                h_new = None
                decs = []
                for s in range(2):
                    hh = 2 * p + s
                    msk = lo_mask if s == 0 else jnp.logical_not(lo_mask)
                    col = cum[:, hh:hh + 1]
                    row = cum_t[hh:hh + 1, :]
                    dt_row = dt_t[hh:hh + 1, :]
                    dec = jnp.exp(jnp.where(causal, col - row, NEG))
                    w_mat = (cbm * dec * dt_row).astype(BF16)
                    c_exp = (c_g * jnp.exp(col)).astype(BF16)
                    x_s = jnp.where(msk, xp, 0.0).astype(BF16)
                    h_s = jnp.where(msk, h_old, 0.0).astype(BF16)
                    y_acc = y_acc + _dot(w_mat, x_s) + _dot(c_exp, h_s)
                    last = cum[q_len - 1:q_len, hh:hh + 1]
                    b_s = (b_gt * (jnp.exp(last - row) * dt_row)).astype(BF16)
                    upd = _dot(b_s, x_s)
                    h_new = upd if h_new is None else h_new + upd
                    decs.append(jnp.exp(last))
                dec_row = jnp.where(lo_mask, decs[0], decs[1])
                ht[p] = h_old * dec_row + h_new
                y_ref[rs, ps] = y_acc

    @pl.when(i == pl.num_programs(1) - 1)
    def _():
        for p in range(SSD_N_HEADS // 2):
            sst_ref[LANES * p:LANES * (p + 1), :] = ht[p].T


def _ssd_prompt_scan(xbc, dt_raw, small, layer, batch, seq):
    t_len, q_len = SSD_PROMPT_T, SSD_PROMPT_Q
    nt = seq // t_len
    tok = lambda b, i: (b * nt + i, 0)
    return pl.pallas_call(
        functools.partial(_ssd_prompt_kernel, t_len=t_len, q_len=q_len),
        grid=(batch, nt),
        in_specs=[pl.BlockSpec((t_len, SSD_CONV_DIM), tok),
                  pl.BlockSpec((t_len, SSD_DT_PAD), tok)] + _ssd_small_specs(layer),
        out_specs=[pl.BlockSpec((t_len, SSD_D_INNER), tok),
                   pl.BlockSpec((None, SSD_CONV_W - 1, SSD_CONV_DIM), lambda b, i: (b, 0, 0)),
                   pl.BlockSpec((None, SSD_N_HEADS * SSD_HEAD_DIM, SSD_D_STATE), lambda b, i: (b, 0, 0))],
        out_shape=[jax.ShapeDtypeStruct((batch * seq, SSD_D_INNER), F32),
                   jax.ShapeDtypeStruct((batch, SSD_CONV_W - 1, SSD_CONV_DIM), F32),
                   jax.ShapeDtypeStruct((batch, SSD_N_HEADS * SSD_HEAD_DIM, SSD_D_STATE), F32)],
        scratch_shapes=[pltpu.VMEM((t_len + SUBLANES, SSD_CONV_DIM), F32),
                        pltpu.VMEM((t_len, SSD_CONV_DIM), F32),
                        pltpu.VMEM((SSD_N_HEADS // 2, SSD_D_STATE, LANES), F32)],
        compiler_params=_params(2),
        name="ssd_prompt_scan",
    )(xbc, dt_raw, *small)


def _hgrn_gates(q_raw, fz, lb, lbc):
    e = jnp.exp(-jnp.abs(fz))
    r = 1.0 / (1.0 + e)
    er = e * r
    pos = fz >= 0.0
    sig_pos = jnp.where(pos, r, er)
    sig_neg = jnp.where(pos, er, r)
    logf = jnp.log(sig_pos + lbc * sig_neg)
    return _silu(q_raw), (1.0 - lb) * sig_neg, logf


def _hgrn_prompt_kernel(q_ref, f_ref, v_ref, lb_ref, lbc_ref, o_ref, st_ref,
                        qs, ks, cs, stt, *, t_len, c_len):
    i = pl.program_id(1)
    nh = HGRN_N_HEADS
    n_chunks = t_len // c_len

    @pl.when(i == 0)
    def _():
        stt[...] = jnp.zeros_like(stt)

    ri = lax.broadcasted_iota(jnp.int32, (c_len, c_len), 0)
    ci = lax.broadcasted_iota(jnp.int32, (c_len, c_len), 1)
    causal = ri >= ci
    tril = jnp.where(causal, 1.0, 0.0).astype(BF16)
    fsub = HGRN_FAST_SUB
    worst = jnp.zeros((1, LANES), F32)
    for h in range(nh):
        hs = slice(LANES * h, LANES * (h + 1))
        q, k, logf = _hgrn_gates(q_ref[:, hs], f_ref[:, hs], lb_ref[:, hs], lbc_ref[:, hs])
        qs[:, hs] = q
        ks[:, hs] = k
        for c in range(n_chunks):
            cum = _cumsum_rows(tril, logf[c * c_len:(c + 1) * c_len])
            cs[c * c_len:(c + 1) * c_len, hs] = cum
            prev = jnp.zeros((1, LANES), F32)
            for a in range(c_len // fsub):
                end = cum[(a + 1) * fsub - 1:(a + 1) * fsub, :]
                worst = jnp.maximum(worst, prev - end)
                prev = end
    fast = jnp.max(worst) <= HGRN_FAST_RANGE

    row_c = lax.broadcasted_iota(jnp.int32, (c_len, LANES), 0)

    def chunk_head(h, c, att_fn, o_extra):
        hs = slice(LANES * h, LANES * (h + 1))
        rs = slice(c * c_len, (c + 1) * c_len)
        qh, kh, vh, cum = qs[rs, hs], ks[rs, hs], v_ref[rs, hs], cs[rs, hs]
        st_old = stt[h]
        vb16 = vh.astype(BF16)
        o = lax.dot_general((qh * jnp.exp(cum)).astype(BF16), st_old.astype(BF16), _NT,
                            preferred_element_type=F32)
        o = o + _dot(att_fn(qh, kh, cum).astype(BF16), vb16)
        if o_extra is not None:
            o = o + o_extra[rs]
        o_ref[rs, hs] = o
        last = cum[c_len - 1:c_len, :]
        k_hat = (kh * jnp.exp(last - cum)).astype(BF16)
        upd = lax.dot_general(vb16, k_hat, _TN, preferred_element_type=F32)
        stt[h] = st_old * jnp.exp(last) + upd

    def cross_level(qh, kh, cum, s):
        ref_rows = jnp.concatenate(
            [jnp.broadcast_to(cum[blk * 2 * s + s - 1:blk * 2 * s + s, :], (2 * s, LANES))
             for blk in range(c_len // (2 * s))], axis=0)
        second = (row_c // s) % 2 == 1
        e = jnp.exp(jnp.where(second, cum - ref_rows, ref_rows - cum))
        q_t = jnp.where(second, qh * e, 0.0).astype(BF16)
        k_t = jnp.where(second, 0.0, kh * e).astype(BF16)
        part = lax.dot_general(q_t, k_t, _NT, preferred_element_type=F32)
        if 2 * s < c_len:
            part = jnp.where((ri // (2 * s)) == (ci // (2 * s)), part, 0.0)
        return part

    def levels_down_to(sub):
        out = []
        s = c_len // 2
        while s >= sub:
            out.append(s)
            s //= 2
        return out

    @pl.when(fast)
    def _():
        def att_fast(qh, kh, cum):
            start = jnp.concatenate(
                [jnp.zeros((fsub, LANES), F32)] +
                [jnp.broadcast_to(cum[a * fsub - 1:a * fsub, :], (fsub, LANES))
                 for a in range(1, c_len // fsub)], axis=0)
            q_d = (qh * jnp.exp(cum - start)).astype(BF16)
            k_d = (kh * jnp.exp(start - cum)).astype(BF16)
            att = lax.dot_general(q_d, k_d, _NT, preferred_element_type=F32)
            att = jnp.where(causal & ((ri // fsub) == (ci // fsub)), att, 0.0)
            for s in levels_down_to(fsub):
                att = att + cross_level(qh, kh, cum, s)
            return att

        for h in range(nh):
            for c in range(n_chunks):
                chunk_head(h, c, att_fast, None)

    @pl.when(jnp.logical_not(fast))
    def _():
        sub = HGRN_SUB
        row_in_sub = lax.broadcasted_iota(jnp.int32, (t_len, LANES), 0) % sub

        def att_general(qh, kh, cum):
            att = jnp.zeros((c_len, c_len), F32)
            for s in levels_down_to(sub):
                att = att + cross_level(qh, kh, cum, s)
            return att

        for h in range(nh):
            hs = slice(LANES * h, LANES * (h + 1))
            q_all = qs[:, hs]
            cum_all = cs[:, hs]
            o_diag = jnp.zeros((t_len, LANES), F32)
            for j in range(sub):
                def rows(ref):
                    return jnp.concatenate(
                        [jnp.broadcast_to(ref[sub * a + j:sub * a + j + 1, hs], (sub, LANES))
                         for a in range(t_len // sub)], axis=0)
                k_b, cum_b, v_b = rows(ks), rows(cs), rows(v_ref)
                arg = jnp.where(row_in_sub >= j, cum_all - cum_b, NEG)
                z = q_all * k_b * jnp.exp(arg)
                o_diag = o_diag + jnp.sum(z, axis=-1, keepdims=True) * v_b
            for c in range(n_chunks):
                chunk_head(h, c, att_general, o_diag)

    @pl.when(i == pl.num_programs(1) - 1)
    def _():
        for h in range(nh):
            st_ref[h] = stt[h].T


def _hgrn_prompt_scan(qr, fz, v, lb_all, lbc_all, layer, batch, seq):
    t_len, c_len = HGRN_PROMPT_T, HGRN_PROMPT_C
    nt = seq // t_len
    tok = lambda b, i: (b * nt + i, 0)
    return pl.pallas_call(
        functools.partial(_hgrn_prompt_kernel, t_len=t_len, c_len=c_len),
        grid=(batch, nt),
        in_specs=[pl.BlockSpec((t_len, HGRN_F), tok),
                  pl.BlockSpec((t_len, HGRN_F), tok),
                  pl.BlockSpec((t_len, D_MODEL), tok),
                  _layer_resident((1, HGRN_F), layer),
                  _layer_resident((1, HGRN_F), layer)],
        out_specs=[pl.BlockSpec((t_len, D_MODEL), tok),
                   pl.BlockSpec((None, HGRN_N_HEADS, HGRN_DK, HGRN_DV), lambda b, i: (b, 0, 0, 0))],
        out_shape=[jax.ShapeDtypeStruct((batch * seq, D_MODEL), F32),
                   jax.ShapeDtypeStruct((batch, HGRN_N_HEADS, HGRN_DK, HGRN_DV), F32)],
        scratch_shapes=[pltpu.VMEM((t_len, HGRN_F), F32),
                        pltpu.VMEM((t_len, HGRN_F), F32),
                        pltpu.VMEM((t_len, HGRN_F), F32),
                        pltpu.VMEM((HGRN_N_HEADS, HGRN_DV, HGRN_DK), F32)],
        compiler_params=_params(2),
        name="hgrn_prompt_scan",
    )(qr, fz, v, lb_all, lbc_all)


def _state_specs(block, layer, n_layers, prev):
    nd = len(block) - 1
    zeros = (0,) * nd
    in_spec = pl.BlockSpec((None,) + block, lambda i, l: (layer, i) + zeros)
    if prev is None:
        out_spec = pl.BlockSpec((None,) + block, lambda i, l: (l, i) + zeros)
        return in_spec, out_spec, n_layers
    out_spec = pl.BlockSpec((None,) + block, lambda i, l: (layer, i) + zeros)
    return in_spec, out_spec, 1


def _ssd_sample_body(xbc_ref, dt_ref, cst_ref, sst_ref, cw_ref, cb_ref, dtb_ref, a_ref, drep_ref,
                     e16_ref, e32_ref, y_ref, ncst_ref, nsst_ref, xc, xh, yi, n_tok, bb):
    nw = SSD_CONV_W
    ext = [cst_ref[s] for s in range(nw - 1)] + [xbc_ref[t] for t in range(n_tok)]
    for t in range(n_tok):
        acc = cb_ref[...] + ext[t] * cw_ref[0:1, :]
        for w in range(1, nw):
            acc = acc + ext[t + w] * cw_ref[w:w + 1, :]
        xc[t] = _silu(acc)
    for s in range(nw - 1):
        ncst_ref[s] = ext[n_tok + s]

    a_row = a_ref[...]
    dts, cums = [], []
    run = None
    for t in range(n_tok):
        dt = _softplus(dt_ref[t] + dtb_ref[...])
        run = dt * a_row if run is None else run + dt * a_row
        dts.append(dt)
        cums.append(run)

    lane = lax.broadcasted_iota(jnp.int32, (bb, LANES), 1)
    rpg = SSD_N_HEADS // SSD_N_GROUPS

    def group_dot(t, j):
        out = None
        for g in reversed(range(SSD_N_GROUPS)):
            cg = xc[t, :, SSD_D_INNER + SSD_GN + LANES * g:SSD_D_INNER + SSD_GN + LANES * (g + 1)]
            bg = xc[j, :, SSD_D_INNER + LANES * g:SSD_D_INNER + LANES * (g + 1)]
            sg = jnp.sum(cg * bg, axis=-1, keepdims=True)
            out = jnp.broadcast_to(sg, (bb, LANES)) if out is None else jnp.where(lane < rpg * (g + 1), sg, out)
        return out

    pairs = [(t, j) for t in range(n_tok) for j in range(t + 1)]
    w_rows = [group_dot(t, j) * jnp.exp(cums[t] - cums[j]) * dts[j] for (t, j) in pairs]
    tail_rows = [jnp.exp(cums[n_tok - 1] - cums[j]) * dts[j] for j in range(n_tok)]
    we = _dot(jnp.concatenate(w_rows + tail_rows, axis=0).astype(BF16), e16_ref[...])
    ce = _dot_exact(jnp.concatenate([jnp.exp(cums[t]) for t in range(n_tok)], axis=0), e32_ref[...])

    for j in range(n_tok):
        r0 = (len(pairs) + j) * bb
        xh[j] = we[r0:r0 + bb] * xc[j, :, 0:SSD_D_INNER]

    dec = jnp.exp(cums[n_tok - 1])
    gw = SSD_D_INNER // SSD_N_GROUPS
    for b in range(bb):
        for g in range(SSD_N_GROUPS):
            cb_ = xc[:, b, SSD_D_INNER + SSD_GN + LANES * g:SSD_D_INNER + SSD_GN + LANES * (g + 1)]
            bb_ = xc[:, b, SSD_D_INNER + LANES * g:SSD_D_INNER + LANES * (g + 1)]
            h0 = sst_ref[b, gw * g:gw * (g + 1), :]
            yi[:, b, gw * g:gw * (g + 1)] = lax.dot_general(
                cb_.astype(BF16), h0.astype(BF16), _NT, preferred_element_type=F32)
            xhb = xh[:, b, gw * g:gw * (g + 1)]
            upd = lax.dot_general(xhb.astype(BF16), bb_.astype(BF16), _TN, preferred_element_type=F32)
            for r in range(rpg):
                hh = rpg * g + r
                d = dec[b:b + 1, hh:hh + 1]
                rows = slice(SSD_HEAD_DIM * r, SSD_HEAD_DIM * (r + 1))
                nsst_ref[b, gw * g + SSD_HEAD_DIM * r:gw * g + SSD_HEAD_DIM * (r + 1), :] = h0[rows] * d + upd[rows]

    for t in range(n_tok):
        xt = xc[t, :, 0:SSD_D_INNER]
        y = xt * drep_ref[...] + yi[t] * ce[t * bb:(t + 1) * bb]
        for j in range(t + 1):
            r0 = pairs.index((t, j)) * bb
            y = y + we[r0:r0 + bb] * xc[j, :, 0:SSD_D_INNER]
        y_ref[t] = y


def _ssd_sample_kernel(*refs, n_tok, bb, has_prev):
    if has_prev:
        refs = refs[:11] + refs[12:]
    nsst_ref = refs[13]
    fill = pl.program_id(1)

    @pl.when(fill == 0)
    def _():
        _ssd_sample_body(*refs, n_tok, bb)

    if not has_prev:
        @pl.when(fill > 0)
        def _():
            nsst_ref[...] = jnp.zeros_like(nsst_ref)


def _ssd_sample_scan(xbc, dt_raw, cst, sst_all, prev, small, e16, e32, layer, n_tok, batch):
    bb = SAMPLE_BB
    rows = SSD_N_HEADS * SSD_HEAD_DIM
    n_layers = sst_all.shape[0]
    tb = lambda i, l: (0, i, 0)
    st_in, st_out, n_fill = _state_specs((bb, rows, SSD_D_STATE), layer, n_layers, prev)
    in_specs = [pl.BlockSpec((n_tok, bb, SSD_CONV_DIM), tb),
                pl.BlockSpec((n_tok, bb, SSD_DT_PAD), tb),
                pl.BlockSpec((SSD_CONV_W - 1, bb, SSD_CONV_DIM), tb),
                st_in] + _ssd_small_specs(layer) + [
                _resident((LANES, SSD_D_INNER)),
                _resident((LANES, SSD_D_INNER))]
    args = [xbc, dt_raw, cst, sst_all, *small, e16, e32]
    aliases = {}
    if prev is not None:
        in_specs.append(pl.BlockSpec(memory_space=pl.ANY))
        args.append(prev)
        aliases = {len(args) - 1: 2}
    return pl.pallas_call(
        functools.partial(_ssd_sample_kernel, n_tok=n_tok, bb=bb, has_prev=prev is not None),
        grid=(batch // bb, n_fill),
        in_specs=in_specs,
        out_specs=[pl.BlockSpec((n_tok, bb, SSD_D_INNER), tb),
                   pl.BlockSpec((SSD_CONV_W - 1, bb, SSD_CONV_DIM), tb),
                   st_out],
        out_shape=[jax.ShapeDtypeStruct((n_tok, batch, SSD_D_INNER), F32),
                   jax.ShapeDtypeStruct((SSD_CONV_W - 1, batch, SSD_CONV_DIM), F32),
                   jax.ShapeDtypeStruct((n_layers, batch, rows, SSD_D_STATE), F32)],
        scratch_shapes=[pltpu.VMEM((n_tok, bb, SSD_CONV_DIM), F32),
                        pltpu.VMEM((n_tok, bb, SSD_D_INNER), F32),
                        pltpu.VMEM((n_tok, bb, SSD_D_INNER), F32)],
        input_output_aliases=aliases,
        compiler_params=_params(2),
        name="ssd_sample_scan",
    )(*args)


def _hgrn_sample_body(q_ref, f_ref, v_ref, st_ref, lb_ref, lbc_ref, o_ref, nst_ref,
                      qe, kh, oi, n_tok, bb):
    nh = HGRN_N_HEADS
    qs, ks, cums = [], [], []
    run = None
    for t in range(n_tok):
        q, k, logf = _hgrn_gates(q_ref[t], f_ref[t], lb_ref[...], lbc_ref[...])
        run = logf if run is None else run + logf
        qs.append(q)
        ks.append(k)
        cums.append(run)
    last = cums[n_tok - 1]
    for t in range(n_tok):
        qe[t] = qs[t] * jnp.exp(cums[t])
        kh[t] = ks[t] * jnp.exp(last - cums[t])
    dec = jnp.exp(last)

    for h in range(nh):
        hs = slice(LANES * h, LANES * (h + 1))
        dec_t = dec[:, hs].T
        for b in range(bb):
            s0 = st_ref[b, h]
            oi[:, b, hs] = _dot(qe[:, b, hs].astype(BF16), s0.astype(BF16))
            upd = lax.dot_general(kh[:, b, hs].astype(BF16), v_ref[:, b, hs].astype(BF16), _TN,
                                  preferred_element_type=F32)
            nst_ref[b, h] = s0 * dec_t[:, b:b + 1] + upd

    for t in range(n_tok):
        o = oi[t]
        for j in range(t + 1):
            z = qs[t] * ks[j] * jnp.exp(cums[t] - cums[j])
            vj = v_ref[j]
            parts = []
            for h in range(nh):
                hs = slice(LANES * h, LANES * (h + 1))
                parts.append(jnp.sum(z[:, hs], axis=-1, keepdims=True) * vj[:, hs])
            o = o + jnp.concatenate(parts, axis=-1)
        o_ref[t] = o


def _hgrn_sample_kernel(*refs, n_tok, bb, has_prev):
    if has_prev:
        refs = refs[:6] + refs[7:]
    nst_ref = refs[7]
    fill = pl.program_id(1)

    @pl.when(fill == 0)
    def _():
        _hgrn_sample_body(*refs, n_tok, bb)

    if not has_prev:
        @pl.when(fill > 0)
        def _():
            nst_ref[...] = jnp.zeros_like(nst_ref)


def _hgrn_sample_scan(qr, fz, v, st_all, prev, lb_all, lbc_all, layer, n_tok, batch):
    bb = SAMPLE_BB
    n_layers = st_all.shape[0]
    tb = lambda i, l: (0, i, 0)
    st_in, st_out, n_fill = _state_specs((bb, HGRN_N_HEADS, HGRN_DK, HGRN_DV), layer, n_layers, prev)
    in_specs = [pl.BlockSpec((n_tok, bb, HGRN_F), tb),
                pl.BlockSpec((n_tok, bb, HGRN_F), tb),
                pl.BlockSpec((n_tok, bb, D_MODEL), tb),
                st_in,
                _layer_resident((1, HGRN_F), layer),
                _layer_resident((1, HGRN_F), layer)]
    args = [qr, fz, v, st_all, lb_all, lbc_all]
    aliases = {}
    if prev is not None:
        in_specs.append(pl.BlockSpec(memory_space=pl.ANY))
        args.append(prev)
        aliases = {len(args) - 1: 1}
    return pl.pallas_call(
        functools.partial(_hgrn_sample_kernel, n_tok=n_tok, bb=bb, has_prev=prev is not None),
        grid=(batch // bb, n_fill),
        in_specs=in_specs,
        out_specs=[pl.BlockSpec((n_tok, bb, D_MODEL), tb), st_out],
        out_shape=[jax.ShapeDtypeStruct((n_tok, batch, D_MODEL), F32),
                   jax.ShapeDtypeStruct((n_layers, batch, HGRN_N_HEADS, HGRN_DK, HGRN_DV), F32)],
        scratch_shapes=[pltpu.VMEM((n_tok, bb, HGRN_F), F32),
                        pltpu.VMEM((n_tok, bb, HGRN_F), F32),
                        pltpu.VMEM((n_tok, bb, D_MODEL), F32)],
        input_output_aliases=aliases,
        compiler_params=_params(2),
        name="hgrn_sample_scan",
    )(*args)


def _prep_params(norm_mix_w, norm_mlp_w, norm_f_w, ssd_w_in, ssd_conv_w, ssd_conv_b, ssd_dt_bias,
                 ssd_a_log, ssd_d, ssd_norm_w, ssd_w_out, hgrn_w_in, hgrn_lb_raw, hgrn_norm_w,
                 hgrn_w_out, mlp_w_up, mlp_w_down):
    la = ssd_w_in.shape[0]
    pad_dt = SSD_DT_PAD - SSD_N_HEADS
    p = {}
    p["norm_mix"] = norm_mix_w.reshape(DEPTH, 1, D_MODEL)
    p["norm_mlp"] = norm_mlp_w.reshape(DEPTH, 1, D_MODEL)
    p["norm_f"] = norm_f_w.reshape(1, D_MODEL)
    p["ssd_w_in"] = jnp.pad(ssd_w_in, ((0, 0), (0, 0), (0, pad_dt))).astype(BF16)
    p["ssd_small"] = (
        ssd_conv_w,
        ssd_conv_b.reshape(la, 1, SSD_CONV_DIM),
        jnp.pad(ssd_dt_bias, ((0, 0), (0, pad_dt))).reshape(la, 1, SSD_DT_PAD),
        jnp.pad(-jnp.exp(ssd_a_log.astype(F32)), ((0, 0), (0, pad_dt))).reshape(la, 1, SSD_DT_PAD),
        jnp.repeat(ssd_d.astype(F32), SSD_HEAD_DIM, axis=1).reshape(la, 1, SSD_D_INNER))
    p["ssd_norm"] = ssd_norm_w.reshape(la, 1, SSD_D_INNER)
    p["ssd_w_out"] = ssd_w_out.astype(BF16)
    p["hgrn_w_in"] = hgrn_w_in.astype(BF16)
    sm = jax.nn.softmax(hgrn_lb_raw.astype(F32), axis=0)
    lb = jnp.cumsum(sm, axis=0) - sm[0]
    lbn = lb.shape[0]
    p["hgrn_lb"] = lb.reshape(lbn, 1, HGRN_F)
    p["hgrn_lbc"] = jnp.maximum(lb, LB_FLOOR).reshape(lbn, 1, HGRN_F)
    p["hgrn_norm"] = hgrn_norm_w.reshape(lbn, 1, D_MODEL)
    p["hgrn_w_out"] = hgrn_w_out.astype(BF16)
    p["mlp_w_up"] = mlp_w_up.astype(BF16)
    p["mlp_w_down"] = mlp_w_down.astype(BF16)
    head_of_lane = jnp.arange(SSD_D_INNER, dtype=jnp.int32) // SSD_HEAD_DIM
    expand = (jnp.arange(LANES, dtype=jnp.int32)[:, None] == head_of_lane[None, :])
    p["expand16"] = expand.astype(BF16)
    p["expand32"] = expand.astype(F32)
    return p


def _trunk(h, p, prompt, batch, seq, conv_states, ssm_states, hgrn_states):
    tm = min(TOKEN_TILE, h.shape[0])
    new_conv, new_ssm, new_hgrn = [], [], []
    ssm_all, hgrn_all = None, None
    for layer in range(DEPTH):
        j = layer // 2
        if layer % 2 == 0:
            z, xbc, dt_raw = _norm_matmul(h, p["norm_mix"], layer, p["ssd_w_in"], j,
                                          (SSD_D_INNER, SSD_CONV_DIM, SSD_DT_PAD), tm, "ssd_in_proj")
            if prompt:
                y, cst, sst = _ssd_prompt_scan(xbc, dt_raw, p["ssd_small"], j, batch, seq)
                new_ssm.append(sst.reshape(batch, SSD_N_HEADS, SSD_HEAD_DIM, SSD_D_STATE))
            else:
                n_l = ssm_states.shape[0]
                y, cst, ssm_all = _ssd_sample_scan(
                    xbc.reshape(seq, batch, SSD_CONV_DIM), dt_raw.reshape(seq, batch, SSD_DT_PAD),
                    jnp.swapaxes(conv_states[j], 0, 1),
                    ssm_states.reshape(n_l, batch, SSD_N_HEADS * SSD_HEAD_DIM, SSD_D_STATE), ssm_all,
                    p["ssd_small"], p["expand16"], p["expand32"], j, seq, batch)
                y = y.reshape(seq * batch, SSD_D_INNER)
                cst = jnp.swapaxes(cst, 0, 1)
            new_conv.append(cst)
            h = _gate_out(y, z, p["ssd_norm"], p["ssd_w_out"], j, h, SSD_N_GROUPS, True, tm, "ssd_out_proj")
        else:
            qr, fz, v, g = _norm_matmul(h, p["norm_mix"], layer, p["hgrn_w_in"], j,
                                        (HGRN_F, HGRN_F, D_MODEL, D_MODEL), tm, "hgrn_in_proj")
            if prompt:
                o, st = _hgrn_prompt_scan(qr, fz, v, p["hgrn_lb"], p["hgrn_lbc"], j, batch, seq)
                new_hgrn.append(st)
            else:
                o, hgrn_all = _hgrn_sample_scan(
                    qr.reshape(seq, batch, HGRN_F), fz.reshape(seq, batch, HGRN_F),
                    v.reshape(seq, batch, D_MODEL), hgrn_states, hgrn_all,
                    p["hgrn_lb"], p["hgrn_lbc"], j, seq, batch)
                o = o.reshape(seq * batch, D_MODEL)
            h = _gate_out(o, g, p["hgrn_norm"], p["hgrn_w_out"], j, h, HGRN_N_HEADS, False, tm, "hgrn_out_proj")
        h = _mlp(h, p["norm_mlp"], p["mlp_w_up"], p["mlp_w_down"], layer, p["norm_f"],
                 layer == DEPTH - 1, tm, "mlp")
    if prompt:
        return h, jnp.stack(new_conv), jnp.stack(new_ssm), jnp.stack(new_hgrn)
    n_l = ssm_states.shape[0]
    return (h, jnp.stack(new_conv),
            ssm_all.reshape(n_l, batch, SSD_N_HEADS, SSD_HEAD_DIM, SSD_D_STATE), hgrn_all)


def kernel(x_prompt, x_sample, state_ssd_conv, state_ssd_ssm, state_hgrn, norm_mix_w, norm_mlp_w, norm_f_w,
           ssd_w_in, ssd_conv_w, ssd_conv_b, ssd_dt_bias, ssd_a_log, ssd_d, ssd_norm_w, ssd_w_out,
           hgrn_w_in, hgrn_lb_raw, hgrn_norm_w, hgrn_w_out, mlp_w_up, mlp_w_down):
    p = _prep_params(norm_mix_w, norm_mlp_w, norm_f_w, ssd_w_in, ssd_conv_w, ssd_conv_b, ssd_dt_bias,
                     ssd_a_log, ssd_d, ssd_norm_w, ssd_w_out, hgrn_w_in, hgrn_lb_raw, hgrn_norm_w,
                     hgrn_w_out, mlp_w_up, mlp_w_down)
    bp, lp, d = x_prompt.shape
    bs, ls, _ = x_sample.shape
    y_p, conv_p, ssm_p, hgrn_p = _trunk(x_prompt.reshape(bp * lp, d), p, True, bp, lp, None, None, None)
    xs = jnp.swapaxes(x_sample, 0, 1).reshape(ls * bs, d)
    y_s, conv_s, ssm_s, hgrn_s = _trunk(xs, p, False, bs, ls, state_ssd_conv, state_ssd_ssm, state_hgrn)
    y_s = jnp.swapaxes(y_s.reshape(ls, bs, d), 0, 1)
    return (y_p.reshape(bp, lp, d), y_s, conv_p, ssm_p, hgrn_p, conv_s, ssm_s, hgrn_s)
```

```python
import functools

import jax
import jax.numpy as jnp
from jax import lax
from jax.experimental import pallas as pl
from jax.experimental.pallas import tpu as pltpu

F32 = jnp.float32
BF16 = jnp.bfloat16

D_MODEL = 1024
DEPTH = 4
SSD_D_INNER = 2048
SSD_HEAD_DIM = 64
SSD_N_HEADS = 32
SSD_N_GROUPS = 4
SSD_D_STATE = 128
SSD_CONV_W = 4
SSD_GN = SSD_N_GROUPS * SSD_D_STATE
SSD_CONV_DIM = SSD_D_INNER + 2 * SSD_GN
SSD_DT_PAD = 128
HGRN_N_HEADS = 8
HGRN_DK = 128
HGRN_DV = 128
HGRN_F = HGRN_N_HEADS * HGRN_DK
D_FF = 4 * D_MODEL
EPS = 1e-5
LB_FLOOR = 1e-20
NEG = -1e30
LANES = 128
SUBLANES = 8
VMEM_LIMIT = 56 * 1024 * 1024

TOKEN_TILE = 512
SSD_PROMPT_T = 256
SSD_PROMPT_Q = 128
HGRN_PROMPT_T = 128
HGRN_PROMPT_C = 64
HGRN_SUB = 8
HGRN_FAST_SUB = 32
HGRN_FAST_RANGE = 60.0
SAMPLE_BB = 8

_NT = (((1,), (1,)), ((), ()))
_TN = (((0,), (0,)), ((), ()))


def _sigmoid(x):
    return jax.nn.sigmoid(x)


def _silu(x):
    return x * jax.nn.sigmoid(x)


def _softplus(x):
    return jnp.maximum(x, 0.0) + jnp.log1p(jnp.exp(-jnp.abs(x)))


def _rms(x, w):
    ms = jnp.mean(x * x, axis=-1, keepdims=True)
    return x * lax.rsqrt(ms + EPS) * w


def _dot(a, b):
    return jnp.dot(a, b, preferred_element_type=F32)


def _dot_exact(a, b):
    return jnp.dot(a, b, precision=lax.Precision.HIGHEST, preferred_element_type=F32)


def _cumsum_rows(tril, x):
    hi = x.astype(BF16)
    r1 = x - hi.astype(F32)
    mid = r1.astype(BF16)
    lo = (r1 - mid.astype(F32)).astype(BF16)
    return _dot(tril, hi) + _dot(tril, mid) + _dot(tril, lo)


def _params(n_axes):
    return pltpu.CompilerParams(dimension_semantics=("arbitrary",) * n_axes,
                                vmem_limit_bytes=VMEM_LIMIT)


def _resident(shape):
    nd = len(shape)
    return pl.BlockSpec(shape, lambda *_: (0,) * nd, pipeline_mode=pl.Buffered(1))


def _layer_resident(shape, layer):
    nd = len(shape)
    return pl.BlockSpec((None,) + tuple(shape), lambda *_: (layer,) + (0,) * nd,
                        pipeline_mode=pl.Buffered(1))


def _norm_matmul_kernel(x_ref, nw_ref, w_ref, *o_refs):
    u = _rms(x_ref[...], nw_ref[...]).astype(BF16)
    c = 0
    for o_ref in o_refs:
        n = o_ref.shape[-1]
        o_ref[...] = _dot(u, w_ref[:, c:c + n])
        c += n


def _norm_matmul(x, nw_all, nw_layer, w_all, w_layer, splits, tm, name):
    m, d = x.shape
    n_tot = w_all.shape[-1]
    assert sum(splits) == n_tot and m % tm == 0
    return pl.pallas_call(
        _norm_matmul_kernel,
        grid=(m // tm,),
        in_specs=[pl.BlockSpec((tm, d), lambda i: (i, 0)),
                  _layer_resident((1, d), nw_layer),
                  _layer_resident((d, n_tot), w_layer)],
        out_specs=[pl.BlockSpec((tm, n), lambda i: (i, 0)) for n in splits],
        out_shape=[jax.ShapeDtypeStruct((m, n), F32) for n in splits],
        compiler_params=_params(1),
        name=name,
    )(x, nw_all, w_all)


def _gate_out_compute(y_ref, g_ref, nw_ref, w_ref, acc, n_groups, gate_first):
    gs = y_ref.shape[-1] // n_groups
    for k in range(n_groups):
        sl = slice(k * gs, (k + 1) * gs)
        yk = y_ref[:, sl]
        sg = _silu(g_ref[:, sl])
        if gate_first:
            yk = yk * sg
        ms = jnp.mean(yk * yk, axis=-1, keepdims=True)
        yn = yk * lax.rsqrt(ms + EPS) * nw_ref[:, sl]
        if not gate_first:
            yn = yn * sg
        acc = acc + _dot(yn.astype(BF16), w_ref[sl, :])
    return acc


def _gate_out_kernel(y_ref, g_ref, nw_ref, w_ref, h_ref, o_ref, *, n_groups, gate_first):
    o_ref[...] = _gate_out_compute(y_ref, g_ref, nw_ref, w_ref, h_ref[...], n_groups, gate_first)


def _gate_out(y, g, nw_all, w_all, layer, h, n_groups, gate_first, tm, name):
    m, di = y.shape
    d = h.shape[1]
    return pl.pallas_call(
        functools.partial(_gate_out_kernel, n_groups=n_groups, gate_first=gate_first),
        grid=(m // tm,),
        in_specs=[pl.BlockSpec((tm, di), lambda i: (i, 0)),
                  pl.BlockSpec((tm, di), lambda i: (i, 0)),
                  _layer_resident((1, di), layer),
                  _layer_resident((di, d), layer),
                  pl.BlockSpec((tm, d), lambda i: (i, 0))],
        out_specs=pl.BlockSpec((tm, d), lambda i: (i, 0)),
        out_shape=jax.ShapeDtypeStruct((m, d), F32),
        compiler_params=_params(1),
        name=name,
    )(y, g, nw_all, w_all, h)


def _mlp_kernel(x_ref, nw_ref, wu_ref, wd_ref, nf_ref, o_ref, *, final_norm, f_chunk):
    x = x_ref[...]
    u = _rms(x, nw_ref[...]).astype(BF16)
    acc = x
    for f in range(0, D_FF, f_chunk):
        mid = _dot(u, wu_ref[:, f:f + f_chunk])
        mid = jnp.square(jnp.maximum(mid, 0.0)).astype(BF16)
        acc = acc + _dot(mid, wd_ref[f:f + f_chunk, :])
    if final_norm:
        acc = _rms(acc, nf_ref[...])
    o_ref[...] = acc


def _mlp(x, nw_all, wu_all, wd_all, layer, nf, final_norm, tm, name):
    m, d = x.shape
    return pl.pallas_call(
        functools.partial(_mlp_kernel, final_norm=final_norm, f_chunk=1024),
        grid=(m // tm,),
        in_specs=[pl.BlockSpec((tm, d), lambda i: (i, 0)),
                  _layer_resident((1, d), layer),
                  _layer_resident((d, D_FF), layer),
                  _layer_resident((D_FF, d), layer),
                  _resident((1, d))],
        out_specs=pl.BlockSpec((tm, d), lambda i: (i, 0)),
        out_shape=jax.ShapeDtypeStruct((m, d), F32),
        compiler_params=_params(1),
        name=name,
    )(x, nw_all, wu_all, wd_all, nf)


def _ssd_small_specs(layer):
    return [_layer_resident((SSD_CONV_W, SSD_CONV_DIM), layer),
            _layer_resident((1, SSD_CONV_DIM), layer),
            _layer_resident((1, SSD_DT_PAD), layer),
            _layer_resident((1, SSD_DT_PAD), layer),
            _layer_resident((1, SSD_D_INNER), layer)]


def _ssd_project_jobs(h_tile, nmix_ref, win_ref, us, xext, zs, dts, t_len, n_blk=256):
    cdim = SSD_CONV_DIM
    us[...] = _rms(h_tile, nmix_ref[...]).astype(BF16)

    def job(dst, rows, c_dst, c_src, n):
        def run():
            dst[rows, c_dst:c_dst + n] = _dot(us[...], win_ref[:, c_src:c_src + n])
        return run

    every = slice(None)
    jobs = [job(dts, every, 0, SSD_D_INNER + cdim, SSD_DT_PAD)]
    jobs += [job(xext, slice(SUBLANES, SUBLANES + t_len), c, SSD_D_INNER + c, n_blk) for c in range(0, cdim, n_blk)]
    jobs += [job(zs, every, c, c, n_blk) for c in range(0, SSD_D_INNER, n_blk)]
    return jobs


def _ssd_tile(h_ref, xext, xext_next, zs, dts, cw_ref, cb_ref, dtb_ref, a_ref, drep_ref, nrm_ref, wout_ref,
              o_ref, rows_out, cst_ref, xc, ys, ht, t_len, q_len, side_jobs):
    jobs = list(side_jobs)
    cdim = SSD_CONV_DIM
    off = SUBLANES - (SSD_CONV_W - 1)
    for c0 in range(0, cdim, 512):
        cs = slice(c0, c0 + 512)
        acc = cb_ref[:, cs] + xext[off:off + t_len, cs] * cw_ref[0:1, cs]
        for w in range(1, SSD_CONV_W):
            acc = acc + xext[off + w:off + w + t_len, cs] * cw_ref[w:w + 1, cs]
        xc[:, cs] = _silu(acc)
    cst_ref[...] = xext[t_len + off:t_len + SUBLANES, :]
    xext_next[0:SUBLANES, :] = xext[t_len:t_len + SUBLANES, :]

    dt_all = _softplus(dts[...] + dtb_ref[...])
    a_row = a_ref[...]
    ri = lax.broadcasted_iota(jnp.int32, (q_len, q_len), 0)
    ci = lax.broadcasted_iota(jnp.int32, (q_len, q_len), 1)
    causal = ri >= ci
    tril = jnp.where(causal, 1.0, 0.0).astype(BF16)
    lane = lax.broadcasted_iota(jnp.int32, (1, LANES), 1)
    lo_mask = lane < SSD_HEAD_DIM

    for c in range(t_len // q_len):
        rs = slice(c * q_len, (c + 1) * q_len)
        dtc = dt_all[rs]
        cum = _cumsum_rows(tril, dtc * a_row)
        cum_t = cum.T
        dt_t = dtc.T
        for g in range(SSD_N_GROUPS):
            b_g = xc[rs, SSD_D_INNER + LANES * g:SSD_D_INNER + LANES * (g + 1)]
            c_g = xc[rs, SSD_D_INNER + SSD_GN + LANES * g:SSD_D_INNER + SSD_GN + LANES * (g + 1)]
            cbm = lax.dot_general(c_g.astype(BF16), b_g.astype(BF16), _NT, preferred_element_type=F32)
            b_gt = b_g.T
            for m in range(SSD_N_HEADS // SSD_N_GROUPS // 2):
                p = g * 4 + m
                ps = slice(LANES * p, LANES * (p + 1))
                xp = xc[rs, ps]
                h_old = ht[p]
                w_mats, c_exps, b_ss, x_ss, h_ss, decs = [], [], [], [], [], []
                for s in range(2):
                    hh = 2 * p + s
                    msk = lo_mask if s == 0 else jnp.logical_not(lo_mask)
                    col = cum[:, hh:hh + 1]
                    row = cum_t[hh:hh + 1, :]
                    dt_row = dt_t[hh:hh + 1, :]
                    dec = jnp.exp(jnp.where(causal, col - row, NEG))
                    w_mats.append((cbm * dec * dt_row).astype(BF16))
                    c_exps.append((c_g * jnp.exp(col)).astype(BF16))
                    x_ss.append(jnp.where(msk, xp, 0.0).astype(BF16))
                    h_ss.append(jnp.where(msk, h_old, 0.0).astype(BF16))
                    last = cum[q_len - 1:q_len, hh:hh + 1]
                    b_ss.append((b_gt * (jnp.exp(last - row) * dt_row)).astype(BF16))
                    decs.append(jnp.exp(last))
                x_cat = jnp.concatenate(x_ss, axis=0)
                y_acc = (xp * drep_ref[:, ps]
                         + _dot(jnp.concatenate(w_mats + c_exps, axis=1),
                                jnp.concatenate([x_cat] + h_ss, axis=0)))
                dec_row = jnp.where(lo_mask, decs[0], decs[1])
                ht[p] = h_old * dec_row + _dot(jnp.concatenate(b_ss, axis=1), x_cat)
                ys[rs, ps] = y_acc
                if jobs:
                    jobs.pop(0)()

    while jobs:
        jobs.pop(0)()
    o_ref[rows_out, :] = _gate_out_compute(ys, zs, nrm_ref, wout_ref, h_ref[rows_out, :], SSD_N_GROUPS, True)


def _ssd_prompt_kernel(h_ref, hn_ref, nmix_ref, win_ref, cw_ref, cb_ref, dtb_ref, a_ref, drep_ref, nrm_ref, wout_ref,
                       o_ref, cst_ref, sst_ref, us, xa, xb, za, zb, da, db, xc, ys, ht, *, t_len, q_len):
    i = pl.program_id(1)
    rows_a, rows_b = slice(0, t_len), slice(t_len, 2 * t_len)
    rest = (cw_ref, cb_ref, dtb_ref, a_ref, drep_ref, nrm_ref, wout_ref)

    @pl.when(i == 0)
    def _():
        xa[0:SUBLANES, :] = jnp.zeros((SUBLANES, SSD_CONV_DIM), F32)
        ht[...] = jnp.zeros_like(ht)
        for job in _ssd_project_jobs(h_ref[rows_a, :], nmix_ref, win_ref, us, xa, za, da, t_len):
            job()

    jobs_b = _ssd_project_jobs(h_ref[rows_b, :], nmix_ref, win_ref, us, xb, zb, db, t_len)
    _ssd_tile(h_ref, xa, xb, za, da, *rest, o_ref, rows_a, cst_ref, xc, ys, ht, t_len, q_len, jobs_b)
    jobs_a = _ssd_project_jobs(hn_ref[...], nmix_ref, win_ref, us, xa, za, da, t_len)
    _ssd_tile(h_ref, xb, xa, zb, db, *rest, o_ref, rows_b, cst_ref, xc, ys, ht, t_len, q_len, jobs_a)

    @pl.when(i == pl.num_programs(1) - 1)
    def _():
        for p in range(SSD_N_HEADS // 2):
            sst_ref[LANES * p:LANES * (p + 1), :] = ht[p].T


def _ssd_prompt_layer(h, p, layer, j, batch, seq):
    t_len, q_len = SSD_PROMPT_T, SSD_PROMPT_Q
    assert seq % (2 * t_len) == 0
    ns = seq // (2 * t_len)
    tok = lambda b, i: (b * ns + i, 0)
    nxt = lambda b, i: (2 * (b * ns + jnp.minimum(i + 1, ns - 1)), 0)
    n_in = SSD_D_INNER + SSD_CONV_DIM + SSD_DT_PAD
    xbuf = pltpu.VMEM((t_len + SUBLANES, SSD_CONV_DIM), F32)
    zbuf = pltpu.VMEM((t_len, SSD_D_INNER), F32)
    dbuf = pltpu.VMEM((t_len, SSD_DT_PAD), F32)
    return pl.pallas_call(
        functools.partial(_ssd_prompt_kernel, t_len=t_len, q_len=q_len),
        grid=(batch, ns),
        in_specs=[pl.BlockSpec((2 * t_len, D_MODEL), tok),
                  pl.BlockSpec((t_len, D_MODEL), nxt),
                  _layer_resident((1, D_MODEL), layer),
                  _layer_resident((D_MODEL, n_in), j)] + _ssd_small_specs(j) + [
                  _layer_resident((1, SSD_D_INNER), j),
                  _layer_resident((SSD_D_INNER, D_MODEL), j)],
        out_specs=[pl.BlockSpec((2 * t_len, D_MODEL), tok),
                   pl.BlockSpec((None, SSD_CONV_W - 1, SSD_CONV_DIM), lambda b, i: (b, 0, 0)),
                   pl.BlockSpec((None, SSD_N_HEADS * SSD_HEAD_DIM, SSD_D_STATE), lambda b, i: (b, 0, 0))],
        out_shape=[jax.ShapeDtypeStruct((batch * seq, D_MODEL), F32),
                   jax.ShapeDtypeStruct((batch, SSD_CONV_W - 1, SSD_CONV_DIM), F32),
                   jax.ShapeDtypeStruct((batch, SSD_N_HEADS * SSD_HEAD_DIM, SSD_D_STATE), F32)],
        scratch_shapes=[pltpu.VMEM((t_len, D_MODEL), BF16), xbuf, xbuf, zbuf, zbuf, dbuf, dbuf,
                        pltpu.VMEM((t_len, SSD_CONV_DIM), F32),
                        pltpu.VMEM((t_len, SSD_D_INNER), F32),
                        pltpu.VMEM((SSD_N_HEADS // 2, SSD_D_STATE, LANES), F32)],
        compiler_params=_params(2),
        name="ssd_prompt_layer",
    )(h, h, p["norm_mix"], p["ssd_w_in"], *p["ssd_small"], p["ssd_norm"], p["ssd_w_out"])


def _hgrn_gates(q_raw, fz, lb, lbc):
    e = jnp.exp(-jnp.abs(fz))
    r = 1.0 / (1.0 + e)
    er = e * r
    pos = fz >= 0.0
    sig_pos = jnp.where(pos, r, er)
    sig_neg = jnp.where(pos, er, r)
    logf = jnp.log(sig_pos + lbc * sig_neg)
    return _silu(q_raw), (1.0 - lb) * sig_neg, logf


def _hgrn_prompt_kernel(h_ref, nmix_ref, win_ref, lb_ref, lbc_ref, nrm_ref, wout_ref, o_ref, st_ref,
                        q_ref, f_ref, v_ref, g_ref, qs, ks, cs, os_, stt, *, t_len, c_len):
    i = pl.program_id(1)
    nh = HGRN_N_HEADS
    n_chunks = t_len // c_len

    @pl.when(i == 0)
    def _():
        stt[...] = jnp.zeros_like(stt)

    u = _rms(h_ref[...], nmix_ref[...]).astype(BF16)
    f_ref[...] = _dot(u, win_ref[:, HGRN_F:2 * HGRN_F])
    q_ref[...] = _dot(u, win_ref[:, 0:HGRN_F])
    v_ref[...] = _dot(u, win_ref[:, 2 * HGRN_F:2 * HGRN_F + D_MODEL])
    g_ref[...] = _dot(u, win_ref[:, 2 * HGRN_F + D_MODEL:2 * HGRN_F + 2 * D_MODEL])

    ri = lax.broadcasted_iota(jnp.int32, (c_len, c_len), 0)
    ci = lax.broadcasted_iota(jnp.int32, (c_len, c_len), 1)
    causal = ri >= ci
    tril = jnp.where(causal, 1.0, 0.0).astype(BF16)
    fsub = HGRN_FAST_SUB
    worst = jnp.zeros((1, LANES), F32)
    for h in range(nh):
        hs = slice(LANES * h, LANES * (h + 1))
        q, k, logf = _hgrn_gates(q_ref[:, hs], f_ref[:, hs], lb_ref[:, hs], lbc_ref[:, hs])
        qs[:, hs] = q
        ks[:, hs] = k
        for c in range(n_chunks):
            cum = _cumsum_rows(tril, logf[c * c_len:(c + 1) * c_len])
            cs[c * c_len:(c + 1) * c_len, hs] = cum
            prev = jnp.zeros((1, LANES), F32)
            for a in range(c_len // fsub):
                end = cum[(a + 1) * fsub - 1:(a + 1) * fsub, :]
                worst = jnp.maximum(worst, prev - end)
                prev = end
    fast = jnp.max(worst) <= HGRN_FAST_RANGE

    row_c = lax.broadcasted_iota(jnp.int32, (c_len, LANES), 0)

    def chunk_head(h, c, att_fn, o_extra):
        hs = slice(LANES * h, LANES * (h + 1))
        rs = slice(c * c_len, (c + 1) * c_len)
        qh, kh, vh, cum = qs[rs, hs], ks[rs, hs], v_ref[rs, hs], cs[rs, hs]
        st_old = stt[h]
        vb16 = vh.astype(BF16)
        o = lax.dot_general((qh * jnp.exp(cum)).astype(BF16), st_old.astype(BF16), _NT,
                            preferred_element_type=F32)
        o = o + _dot(att_fn(qh, kh, cum).astype(BF16), vb16)
        if o_extra is not None:
            o = o + o_extra[rs]
        os_[rs, hs] = o
        last = cum[c_len - 1:c_len, :]
        k_hat = (kh * jnp.exp(last - cum)).astype(BF16)
        upd = lax.dot_general(vb16, k_hat, _TN, preferred_element_type=F32)
        stt[h] = st_old * jnp.exp(last) + upd

    def cross_level(qh, kh, cum, s):
        ref_rows = jnp.concatenate(
            [jnp.broadcast_to(cum[blk * 2 * s + s - 1:blk * 2 * s + s, :], (2 * s, LANES))
             for blk in range(c_len // (2 * s))], axis=0)
        second = (row_c // s) % 2 == 1
        e = jnp.exp(jnp.where(second, cum - ref_rows, ref_rows - cum))
        q_t = jnp.where(second, qh * e, 0.0).astype(BF16)
        k_t = jnp.where(second, 0.0, kh * e).astype(BF16)
        part = lax.dot_general(q_t, k_t, _NT, preferred_element_type=F32)
        if 2 * s < c_len:
            part = jnp.where((ri // (2 * s)) == (ci // (2 * s)), part, 0.0)
        return part

    def levels_down_to(sub):
        out = []
        s = c_len // 2
        while s >= sub:
            out.append(s)
            s //= 2
        return out

    @pl.when(fast)
    def _():
        def att_fast(qh, kh, cum):
            start = jnp.concatenate(
                [jnp.zeros((fsub, LANES), F32)] +
                [jnp.broadcast_to(cum[a * fsub - 1:a * fsub, :], (fsub, LANES))
                 for a in range(1, c_len // fsub)], axis=0)
            q_d = (qh * jnp.exp(cum - start)).astype(BF16)
            k_d = (kh * jnp.exp(start - cum)).astype(BF16)
            att = lax.dot_general(q_d, k_d, _NT, preferred_element_type=F32)
            att = jnp.where(causal & ((ri // fsub) == (ci // fsub)), att, 0.0)
            for s in levels_down_to(fsub):
                att = att + cross_level(qh, kh, cum, s)
            return att

        for h in range(nh):
            for c in range(n_chunks):
                chunk_head(h, c, att_fast, None)

    @pl.when(jnp.logical_not(fast))
    def _():
        sub = HGRN_SUB
        row_in_sub = lax.broadcasted_iota(jnp.int32, (t_len, LANES), 0) % sub

        def att_general(qh, kh, cum):
            att = jnp.zeros((c_len, c_len), F32)
            for s in levels_down_to(sub):
                att = att + cross_level(qh, kh, cum, s)
            return att

        for h in range(nh):
            hs = slice(LANES * h, LANES * (h + 1))
            q_all = qs[:, hs]
            cum_all = cs[:, hs]
            o_diag = jnp.zeros((t_len, LANES), F32)
            for j in range(sub):
                def rows(ref):
                    return jnp.concatenate(
                        [jnp.broadcast_to(ref[sub * a + j:sub * a + j + 1, hs], (sub, LANES))
                         for a in range(t_len // sub)], axis=0)
                k_b, cum_b, v_b = rows(ks), rows(cs), rows(v_ref)
                arg = jnp.where(row_in_sub >= j, cum_all - cum_b, NEG)
                z = q_all * k_b * jnp.exp(arg)
                o_diag = o_diag + jnp.sum(z, axis=-1, keepdims=True) * v_b
            for c in range(n_chunks):
                chunk_head(h, c, att_general, o_diag)

    o_ref[...] = _gate_out_compute(os_, g_ref, nrm_ref, wout_ref, h_ref[...], nh, False)

    @pl.when(i == pl.num_programs(1) - 1)
    def _():
        for h in range(nh):
            st_ref[h] = stt[h].T


def _hgrn_prompt_layer(h, p, layer, j, batch, seq):
    t_len, c_len = HGRN_PROMPT_T, HGRN_PROMPT_C
    nt = seq // t_len
    tok = lambda b, i: (b * nt + i, 0)
    n_in = 2 * HGRN_F + 2 * D_MODEL
    return pl.pallas_call(
        functools.partial(_hgrn_prompt_kernel, t_len=t_len, c_len=c_len),
        grid=(batch, nt),
        in_specs=[pl.BlockSpec((t_len, D_MODEL), tok),
                  _layer_resident((1, D_MODEL), layer),
                  _layer_resident((D_MODEL, n_in), j),
                  _layer_resident((1, HGRN_F), j),
                  _layer_resident((1, HGRN_F), j),
                  _layer_resident((1, D_MODEL), j),
                  _layer_resident((D_MODEL, D_MODEL), j)],
        out_specs=[pl.BlockSpec((t_len, D_MODEL), tok),
                   pl.BlockSpec((None, HGRN_N_HEADS, HGRN_DK, HGRN_DV), lambda b, i: (b, 0, 0, 0))],
        out_shape=[jax.ShapeDtypeStruct((batch * seq, D_MODEL), F32),
                   jax.ShapeDtypeStruct((batch, HGRN_N_HEADS, HGRN_DK, HGRN_DV), F32)],
        scratch_shapes=[pltpu.VMEM((t_len, HGRN_F), F32),
                        pltpu.VMEM((t_len, HGRN_F), F32),
                        pltpu.VMEM((t_len, D_MODEL), F32),
                        pltpu.VMEM((t_len, D_MODEL), F32),
                        pltpu.VMEM((t_len, HGRN_F), F32),
                        pltpu.VMEM((t_len, HGRN_F), F32),
                        pltpu.VMEM((t_len, HGRN_F), F32),
                        pltpu.VMEM((t_len, D_MODEL), F32),
                        pltpu.VMEM((HGRN_N_HEADS, HGRN_DV, HGRN_DK), F32)],
        compiler_params=_params(2),
        name="hgrn_prompt_layer",
    )(h, p["norm_mix"], p["hgrn_w_in"], p["hgrn_lb"], p["hgrn_lbc"], p["hgrn_norm"], p["hgrn_w_out"])


def _state_specs(block, layer, n_layers, prev):
    nd = len(block) - 1
    zeros = (0,) * nd
    in_spec = pl.BlockSpec((None,) + block, lambda i, l: (layer, i) + zeros)
    if prev is None:
        out_spec = pl.BlockSpec((None,) + block, lambda i, l: (l, i) + zeros)
        return in_spec, out_spec, n_layers
    out_spec = pl.BlockSpec((None,) + block, lambda i, l: (layer, i) + zeros)
    return in_spec, out_spec, 1


def _ssd_sample_body(xbc_ref, dt_ref, cst_ref, sst_ref, cw_ref, cb_ref, dtb_ref, a_ref, drep_ref,
                     e16_ref, e32_ref, y_ref, ncst_ref, nsst_ref, xc, xh, yi, n_tok, bb):
    nw = SSD_CONV_W
    ext = [cst_ref[s] for s in range(nw - 1)] + [xbc_ref[t] for t in range(n_tok)]
    for t in range(n_tok):
        acc = cb_ref[...] + ext[t] * cw_ref[0:1, :]
        for w in range(1, nw):
            acc = acc + ext[t + w] * cw_ref[w:w + 1, :]
        xc[t] = _silu(acc)
    for s in range(nw - 1):
        ncst_ref[s] = ext[n_tok + s]

    a_row = a_ref[...]
    dts, cums = [], []
    run = None
    for t in range(n_tok):
        dt = _softplus(dt_ref[t] + dtb_ref[...])
        run = dt * a_row if run is None else run + dt * a_row
        dts.append(dt)
        cums.append(run)

    lane = lax.broadcasted_iota(jnp.int32, (bb, LANES), 1)
    rpg = SSD_N_HEADS // SSD_N_GROUPS

    def group_dot(t, j):
        out = None
        for g in reversed(range(SSD_N_GROUPS)):
            cg = xc[t, :, SSD_D_INNER + SSD_GN + LANES * g:SSD_D_INNER + SSD_GN + LANES * (g + 1)]
            bg = xc[j, :, SSD_D_INNER + LANES * g:SSD_D_INNER + LANES * (g + 1)]
            sg = jnp.sum(cg * bg, axis=-1, keepdims=True)
            out = jnp.broadcast_to(sg, (bb, LANES)) if out is None else jnp.where(lane < rpg * (g + 1), sg, out)
        return out

    pairs = [(t, j) for t in range(n_tok) for j in range(t + 1)]
    w_rows = [group_dot(t, j) * jnp.exp(cums[t] - cums[j]) * dts[j] for (t, j) in pairs]
    tail_rows = [jnp.exp(cums[n_tok - 1] - cums[j]) * dts[j] for j in range(n_tok)]
    we = _dot(jnp.concatenate(w_rows + tail_rows, axis=0).astype(BF16), e16_ref[...])
    ce = _dot_exact(jnp.concatenate([jnp.exp(cums[t]) for t in range(n_tok)], axis=0), e32_ref[...])

    for j in range(n_tok):
        r0 = (len(pairs) + j) * bb
        xh[j] = we[r0:r0 + bb] * xc[j, :, 0:SSD_D_INNER]

    dec = jnp.exp(cums[n_tok - 1])
    gw = SSD_D_INNER // SSD_N_GROUPS
    for b in range(bb):
        for g in range(SSD_N_GROUPS):
            cb_ = xc[:, b, SSD_D_INNER + SSD_GN + LANES * g:SSD_D_INNER + SSD_GN + LANES * (g + 1)]
            bb_ = xc[:, b, SSD_D_INNER + LANES * g:SSD_D_INNER + LANES * (g + 1)]
            h0 = sst_ref[b, gw * g:gw * (g + 1), :]
            yi[:, b, gw * g:gw * (g + 1)] = lax.dot_general(
                cb_.astype(BF16), h0.astype(BF16), _NT, preferred_element_type=F32)
            xhb = xh[:, b, gw * g:gw * (g + 1)]
            upd = lax.dot_general(xhb.astype(BF16), bb_.astype(BF16), _TN, preferred_element_type=F32)
            for r in range(rpg):
                hh = rpg * g + r
                d = dec[b:b + 1, hh:hh + 1]
                rows = slice(SSD_HEAD_DIM * r, SSD_HEAD_DIM * (r + 1))
                nsst_ref[b, gw * g + SSD_HEAD_DIM * r:gw * g + SSD_HEAD_DIM * (r + 1), :] = h0[rows] * d + upd[rows]

    for t in range(n_tok):
        xt = xc[t, :, 0:SSD_D_INNER]
        y = xt * drep_ref[...] + yi[t] * ce[t * bb:(t + 1) * bb]
        for j in range(t + 1):
            r0 = pairs.index((t, j)) * bb
            y = y + we[r0:r0 + bb] * xc[j, :, 0:SSD_D_INNER]
        y_ref[t] = y


def _ssd_sample_kernel(*refs, n_tok, bb, has_prev):
    if has_prev:
        refs = refs[:11] + refs[12:]
    nsst_ref = refs[13]
    fill = pl.program_id(1)

    @pl.when(fill == 0)
    def _():
        _ssd_sample_body(*refs, n_tok, bb)

    if not has_prev:
        @pl.when(fill > 0)
        def _():
            nsst_ref[...] = jnp.zeros_like(nsst_ref)


def _ssd_sample_scan(xbc, dt_raw, cst, sst_all, prev, small, e16, e32, layer, n_tok, batch):
    bb = SAMPLE_BB
    rows = SSD_N_HEADS * SSD_HEAD_DIM
    n_layers = sst_all.shape[0]
    tb = lambda i, l: (0, i, 0)
    st_in, st_out, n_fill = _state_specs((bb, rows, SSD_D_STATE), layer, n_layers, prev)
    in_specs = [pl.BlockSpec((n_tok, bb, SSD_CONV_DIM), tb),
                pl.BlockSpec((n_tok, bb, SSD_DT_PAD), tb),
                pl.BlockSpec((SSD_CONV_W - 1, bb, SSD_CONV_DIM), tb),
                st_in] + _ssd_small_specs(layer) + [
                _resident((LANES, SSD_D_INNER)),
                _resident((LANES, SSD_D_INNER))]
    args = [xbc, dt_raw, cst, sst_all, *small, e16, e32]
    aliases = {}
    if prev is not None:
        in_specs.append(pl.BlockSpec(memory_space=pl.ANY))
        args.append(prev)
        aliases = {len(args) - 1: 2}
    return pl.pallas_call(
        functools.partial(_ssd_sample_kernel, n_tok=n_tok, bb=bb, has_prev=prev is not None),
        grid=(batch // bb, n_fill),
        in_specs=in_specs,
        out_specs=[pl.BlockSpec((n_tok, bb, SSD_D_INNER), tb),
                   pl.BlockSpec((SSD_CONV_W - 1, bb, SSD_CONV_DIM), tb),
                   st_out],
        out_shape=[jax.ShapeDtypeStruct((n_tok, batch, SSD_D_INNER), F32),
                   jax.ShapeDtypeStruct((SSD_CONV_W - 1, batch, SSD_CONV_DIM), F32),
                   jax.ShapeDtypeStruct((n_layers, batch, rows, SSD_D_STATE), F32)],
        scratch_shapes=[pltpu.VMEM((n_tok, bb, SSD_CONV_DIM), F32),
                        pltpu.VMEM((n_tok, bb, SSD_D_INNER), F32),
                        pltpu.VMEM((n_tok, bb, SSD_D_INNER), F32)],
        input_output_aliases=aliases,
        compiler_params=_params(2),
        name="ssd_sample_scan",
    )(*args)


def _hgrn_sample_body(q_ref, f_ref, v_ref, st_ref, lb_ref, lbc_ref, o_ref, nst_ref,
                      qe, kh, oi, n_tok, bb):
    nh = HGRN_N_HEADS
    qs, ks, cums = [], [], []
    run = None
    for t in range(n_tok):
        q, k, logf = _hgrn_gates(q_ref[t], f_ref[t], lb_ref[...], lbc_ref[...])
        run = logf if run is None else run + logf
        qs.append(q)
        ks.append(k)
        cums.append(run)
    last = cums[n_tok - 1]
    for t in range(n_tok):
        qe[t] = qs[t] * jnp.exp(cums[t])
        kh[t] = ks[t] * jnp.exp(last - cums[t])
    dec = jnp.exp(last)

    for h in range(nh):
        hs = slice(LANES * h, LANES * (h + 1))
        dec_t = dec[:, hs].T
        for b in range(bb):
            s0 = st_ref[b, h]
            oi[:, b, hs] = _dot(qe[:, b, hs].astype(BF16), s0.astype(BF16))
            upd = lax.dot_general(kh[:, b, hs].astype(BF16), v_ref[:, b, hs].astype(BF16), _TN,
                                  preferred_element_type=F32)
            nst_ref[b, h] = s0 * dec_t[:, b:b + 1] + upd

    for t in range(n_tok):
        o = oi[t]
        for j in range(t + 1):
            z = qs[t] * ks[j] * jnp.exp(cums[t] - cums[j])
            vj = v_ref[j]
            parts = []
            for h in range(nh):
                hs = slice(LANES * h, LANES * (h + 1))
                parts.append(jnp.sum(z[:, hs], axis=-1, keepdims=True) * vj[:, hs])
            o = o + jnp.concatenate(parts, axis=-1)
        o_ref[t] = o


def _hgrn_sample_kernel(*refs, n_tok, bb, has_prev):
    if has_prev:
        refs = refs[:6] + refs[7:]
    nst_ref = refs[7]
    fill = pl.program_id(1)

    @pl.when(fill == 0)
    def _():
        _hgrn_sample_body(*refs, n_tok, bb)

    if not has_prev:
        @pl.when(fill > 0)
        def _():
            nst_ref[...] = jnp.zeros_like(nst_ref)


def _hgrn_sample_scan(qr, fz, v, st_all, prev, lb_all, lbc_all, layer, n_tok, batch):
    bb = SAMPLE_BB
    n_layers = st_all.shape[0]
    tb = lambda i, l: (0, i, 0)
    st_in, st_out, n_fill = _state_specs((bb, HGRN_N_HEADS, HGRN_DK, HGRN_DV), layer, n_layers, prev)
    in_specs = [pl.BlockSpec((n_tok, bb, HGRN_F), tb),
                pl.BlockSpec((n_tok, bb, HGRN_F), tb),
                pl.BlockSpec((n_tok, bb, D_MODEL), tb),
                st_in,
                _layer_resident((1, HGRN_F), layer),
                _layer_resident((1, HGRN_F), layer)]
    args = [qr, fz, v, st_all, lb_all, lbc_all]
    aliases = {}
    if prev is not None:
        in_specs.append(pl.BlockSpec(memory_space=pl.ANY))
        args.append(prev)
        aliases = {len(args) - 1: 1}
    return pl.pallas_call(
        functools.partial(_hgrn_sample_kernel, n_tok=n_tok, bb=bb, has_prev=prev is not None),
        grid=(batch // bb, n_fill),
        in_specs=in_specs,
        out_specs=[pl.BlockSpec((n_tok, bb, D_MODEL), tb), st_out],
        out_shape=[jax.ShapeDtypeStruct((n_tok, batch, D_MODEL), F32),
                   jax.ShapeDtypeStruct((n_layers, batch, HGRN_N_HEADS, HGRN_DK, HGRN_DV), F32)],
        scratch_shapes=[pltpu.VMEM((n_tok, bb, HGRN_F), F32),
                        pltpu.VMEM((n_tok, bb, HGRN_F), F32),
                        pltpu.VMEM((n_tok, bb, D_MODEL), F32)],
        input_output_aliases=aliases,
        compiler_params=_params(2),
        name="hgrn_sample_scan",
    )(*args)


def _prep_params(norm_mix_w, norm_mlp_w, norm_f_w, ssd_w_in, ssd_conv_w, ssd_conv_b, ssd_dt_bias,
                 ssd_a_log, ssd_d, ssd_norm_w, ssd_w_out, hgrn_w_in, hgrn_lb_raw, hgrn_norm_w,
                 hgrn_w_out, mlp_w_up, mlp_w_down):
    la = ssd_w_in.shape[0]
    pad_dt = SSD_DT_PAD - SSD_N_HEADS
    p = {}
    p["norm_mix"] = norm_mix_w.reshape(DEPTH, 1, D_MODEL)
    p["norm_mlp"] = norm_mlp_w.reshape(DEPTH, 1, D_MODEL)
    p["norm_f"] = norm_f_w.reshape(1, D_MODEL)
    p["ssd_w_in"] = jnp.pad(ssd_w_in, ((0, 0), (0, 0), (0, pad_dt))).astype(BF16)
    p["ssd_small"] = (
        ssd_conv_w,
        ssd_conv_b.reshape(la, 1, SSD_CONV_DIM),
        jnp.pad(ssd_dt_bias, ((0, 0), (0, pad_dt))).reshape(la, 1, SSD_DT_PAD),
        jnp.pad(-jnp.exp(ssd_a_log.astype(F32)), ((0, 0), (0, pad_dt))).reshape(la, 1, SSD_DT_PAD),
        jnp.repeat(ssd_d.astype(F32), SSD_HEAD_DIM, axis=1).reshape(la, 1, SSD_D_INNER))
    p["ssd_norm"] = ssd_norm_w.reshape(la, 1, SSD_D_INNER)
    p["ssd_w_out"] = ssd_w_out.astype(BF16)
    p["hgrn_w_in"] = hgrn_w_in.astype(BF16)
    sm = jax.nn.softmax(hgrn_lb_raw.astype(F32), axis=0)
    lb = jnp.cumsum(sm, axis=0) - sm[0]
    lbn = lb.shape[0]
    p["hgrn_lb"] = lb.reshape(lbn, 1, HGRN_F)
    p["hgrn_lbc"] = jnp.maximum(lb, LB_FLOOR).reshape(lbn, 1, HGRN_F)
    p["hgrn_norm"] = hgrn_norm_w.reshape(lbn, 1, D_MODEL)
    p["hgrn_w_out"] = hgrn_w_out.astype(BF16)
    p["mlp_w_up"] = mlp_w_up.astype(BF16)
    p["mlp_w_down"] = mlp_w_down.astype(BF16)
    head_of_lane = jnp.arange(SSD_D_INNER, dtype=jnp.int32) // SSD_HEAD_DIM
    expand = (jnp.arange(LANES, dtype=jnp.int32)[:, None] == head_of_lane[None, :])
    p["expand16"] = expand.astype(BF16)
    p["expand32"] = expand.astype(F32)
    return p


def _trunk(h, p, prompt, batch, seq, conv_states, ssm_states, hgrn_states):
    tm = min(TOKEN_TILE, h.shape[0])
    new_conv, new_ssm, new_hgrn = [], [], []
    ssm_all, hgrn_all = None, None
    for layer in range(DEPTH):
        j = layer // 2
        if layer % 2 == 0 and prompt:
            h, cst, sst = _ssd_prompt_layer(h, p, layer, j, batch, seq)
            new_conv.append(cst)
            new_ssm.append(sst.reshape(batch, SSD_N_HEADS, SSD_HEAD_DIM, SSD_D_STATE))
        elif layer % 2 == 0:
            z, xbc, dt_raw = _norm_matmul(h, p["norm_mix"], layer, p["ssd_w_in"], j,
                                          (SSD_D_INNER, SSD_CONV_DIM, SSD_DT_PAD), tm, "ssd_in_proj")
            n_l = ssm_states.shape[0]
            y, cst, ssm_all = _ssd_sample_scan(
                xbc.reshape(seq, batch, SSD_CONV_DIM), dt_raw.reshape(seq, batch, SSD_DT_PAD),
                jnp.swapaxes(conv_states[j], 0, 1),
                ssm_states.reshape(n_l, batch, SSD_N_HEADS * SSD_HEAD_DIM, SSD_D_STATE), ssm_all,
                p["ssd_small"], p["expand16"], p["expand32"], j, seq, batch)
            new_conv.append(jnp.swapaxes(cst, 0, 1))
            h = _gate_out(y.reshape(seq * batch, SSD_D_INNER), z, p["ssd_norm"], p["ssd_w_out"], j, h,
                          SSD_N_GROUPS, True, tm, "ssd_out_proj")
        elif prompt:
            h, st = _hgrn_prompt_layer(h, p, layer, j, batch, seq)
            new_hgrn.append(st)
        else:
            qr, fz, v, g = _norm_matmul(h, p["norm_mix"], layer, p["hgrn_w_in"], j,
                                        (HGRN_F, HGRN_F, D_MODEL, D_MODEL), tm, "hgrn_in_proj")
            o, hgrn_all = _hgrn_sample_scan(
                qr.reshape(seq, batch, HGRN_F), fz.reshape(seq, batch, HGRN_F),
                v.reshape(seq, batch, D_MODEL), hgrn_states, hgrn_all,
                p["hgrn_lb"], p["hgrn_lbc"], j, seq, batch)
            h = _gate_out(o.reshape(seq * batch, D_MODEL), g, p["hgrn_norm"], p["hgrn_w_out"], j, h,
                          HGRN_N_HEADS, False, tm, "hgrn_out_proj")
        h = _mlp(h, p["norm_mlp"], p["mlp_w_up"], p["mlp_w_down"], layer, p["norm_f"],
                 layer == DEPTH - 1, tm, "mlp")
    if prompt:
        return h, jnp.stack(new_conv), jnp.stack(new_ssm), jnp.stack(new_hgrn)
    n_l = ssm_states.shape[0]
    return (h, jnp.stack(new_conv),
            ssm_all.reshape(n_l, batch, SSD_N_HEADS, SSD_HEAD_DIM, SSD_D_STATE), hgrn_all)


def kernel(x_prompt, x_sample, state_ssd_conv, state_ssd_ssm, state_hgrn, norm_mix_w, norm_mlp_w, norm_f_w,
           ssd_w_in, ssd_conv_w, ssd_conv_b, ssd_dt_bias, ssd_a_log, ssd_d, ssd_norm_w, ssd_w_out,
           hgrn_w_in, hgrn_lb_raw, hgrn_norm_w, hgrn_w_out, mlp_w_up, mlp_w_down):
    p = _prep_params(norm_mix_w, norm_mlp_w, norm_f_w, ssd_w_in, ssd_conv_w, ssd_conv_b, ssd_dt_bias,
                     ssd_a_log, ssd_d, ssd_norm_w, ssd_w_out, hgrn_w_in, hgrn_lb_raw, hgrn_norm_w,
                     hgrn_w_out, mlp_w_up, mlp_w_down)
    bp, lp, d = x_prompt.shape
    bs, ls, _ = x_sample.shape
    y_p, conv_p, ssm_p, hgrn_p = _trunk(x_prompt.reshape(bp * lp, d), p, True, bp, lp, None, None, None)
    xs = jnp.swapaxes(x_sample, 0, 1).reshape(ls * bs, d)
    y_s, conv_s, ssm_s, hgrn_s = _trunk(xs, p, False, bs, ls, state_ssd_conv, state_ssd_ssm, state_hgrn)
    y_s = jnp.swapaxes(y_s.reshape(ls, bs, d), 0, 1)
    return (y_p.reshape(bp, lp, d), y_s, conv_p, ssm_p, hgrn_p, conv_s, ssm_s, hgrn_s)
```

```python
import functools

import jax
import jax.numpy as jnp
from jax import lax
from jax.experimental import pallas as pl
from jax.experimental.pallas import tpu as pltpu

F32 = jnp.float32
BF16 = jnp.bfloat16

D_MODEL = 1024
DEPTH = 4
SSD_D_INNER = 2048
SSD_HEAD_DIM = 64
SSD_N_HEADS = 32
SSD_N_GROUPS = 4
SSD_D_STATE = 128
SSD_CONV_W = 4
SSD_GN = SSD_N_GROUPS * SSD_D_STATE
SSD_CONV_DIM = SSD_D_INNER + 2 * SSD_GN
SSD_DT_PAD = 128
HGRN_N_HEADS = 8
HGRN_DK = 128
HGRN_DV = 128
HGRN_F = HGRN_N_HEADS * HGRN_DK
D_FF = 4 * D_MODEL
EPS = 1e-5
LB_FLOOR = 1e-20
NEG = -1e30
LANES = 128
SUBLANES = 8
VMEM_LIMIT = 56 * 1024 * 1024

TOKEN_TILE = 512
SSD_PROMPT_T = 256
SSD_PROMPT_Q = 128
HGRN_PROMPT_T = 256
HGRN_PROMPT_C = 64
HGRN_SUB = 8
HGRN_FAST_SUB = 32
HGRN_FAST_RANGE = 60.0
SAMPLE_BB = 8

_NT = (((1,), (1,)), ((), ()))
_TN = (((0,), (0,)), ((), ()))


def _sigmoid(x):
    return jax.nn.sigmoid(x)


def _silu(x):
    return x * jax.nn.sigmoid(x)


def _softplus(x):
    return jnp.maximum(x, 0.0) + jnp.log1p(jnp.exp(-jnp.abs(x)))


def _rms(x, w):
    ms = jnp.mean(x * x, axis=-1, keepdims=True)
    return x * lax.rsqrt(ms + EPS) * w


def _dot(a, b):
    return jnp.dot(a, b, preferred_element_type=F32)


def _dot_exact(a, b):
    return jnp.dot(a, b, precision=lax.Precision.HIGHEST, preferred_element_type=F32)


def _cumsum_rows(tril, x):
    hi = x.astype(BF16)
    r1 = x - hi.astype(F32)
    mid = r1.astype(BF16)
    lo = (r1 - mid.astype(F32)).astype(BF16)
    return _dot(tril, hi) + _dot(tril, mid) + _dot(tril, lo)


def _params(n_axes):
    return pltpu.CompilerParams(dimension_semantics=("arbitrary",) * n_axes,
                                vmem_limit_bytes=VMEM_LIMIT)


def _resident(shape):
    nd = len(shape)
    return pl.BlockSpec(shape, lambda *_: (0,) * nd, pipeline_mode=pl.Buffered(1))


def _layer_resident(shape, layer):
    nd = len(shape)
    return pl.BlockSpec((None,) + tuple(shape), lambda *_: (layer,) + (0,) * nd,
                        pipeline_mode=pl.Buffered(1))


def _norm_matmul_kernel(x_ref, nw_ref, w_ref, *o_refs):
    u = _rms(x_ref[...], nw_ref[...]).astype(BF16)
    c = 0
    for o_ref in o_refs:
        n = o_ref.shape[-1]
        o_ref[...] = _dot(u, w_ref[:, c:c + n])
        c += n


def _norm_matmul(x, nw_all, nw_layer, w_all, w_layer, splits, tm, name):
    m, d = x.shape
    n_tot = w_all.shape[-1]
    assert sum(splits) == n_tot and m % tm == 0
    return pl.pallas_call(
        _norm_matmul_kernel,
        grid=(m // tm,),
        in_specs=[pl.BlockSpec((tm, d), lambda i: (i, 0)),
                  _layer_resident((1, d), nw_layer),
                  _layer_resident((d, n_tot), w_layer)],
        out_specs=[pl.BlockSpec((tm, n), lambda i: (i, 0)) for n in splits],
        out_shape=[jax.ShapeDtypeStruct((m, n), F32) for n in splits],
        compiler_params=_params(1),
        name=name,
    )(x, nw_all, w_all)


def _gate_out_compute(y_ref, g_ref, nw_ref, w_ref, acc, n_groups, gate_first):
    gs = y_ref.shape[-1] // n_groups
    for k in range(n_groups):
        sl = slice(k * gs, (k + 1) * gs)
        yk = y_ref[:, sl]
        sg = _silu(g_ref[:, sl])
        if gate_first:
            yk = yk * sg
        ms = jnp.mean(yk * yk, axis=-1, keepdims=True)
        yn = yk * lax.rsqrt(ms + EPS) * nw_ref[:, sl]
        if not gate_first:
            yn = yn * sg
        acc = acc + _dot(yn.astype(BF16), w_ref[sl, :])
    return acc


def _gate_out_kernel(y_ref, g_ref, nw_ref, w_ref, h_ref, o_ref, *, n_groups, gate_first):
    o_ref[...] = _gate_out_compute(y_ref, g_ref, nw_ref, w_ref, h_ref[...], n_groups, gate_first)


def _gate_out(y, g, nw_all, w_all, layer, h, n_groups, gate_first, tm, name):
    m, di = y.shape
    d = h.shape[1]
    return pl.pallas_call(
        functools.partial(_gate_out_kernel, n_groups=n_groups, gate_first=gate_first),
        grid=(m // tm,),
        in_specs=[pl.BlockSpec((tm, di), lambda i: (i, 0)),
                  pl.BlockSpec((tm, di), lambda i: (i, 0)),
                  _layer_resident((1, di), layer),
                  _layer_resident((di, d), layer),
                  pl.BlockSpec((tm, d), lambda i: (i, 0))],
        out_specs=pl.BlockSpec((tm, d), lambda i: (i, 0)),
        out_shape=jax.ShapeDtypeStruct((m, d), F32),
        compiler_params=_params(1),
        name=name,
    )(y, g, nw_all, w_all, h)


def _mlp_kernel(x_ref, nw_ref, wu_ref, wd_ref, nf_ref, o_ref, *, final_norm, f_chunk):
    x = x_ref[...]
    u = _rms(x, nw_ref[...]).astype(BF16)
    acc = x
    for f in range(0, D_FF, f_chunk):
        mid = _dot(u, wu_ref[:, f:f + f_chunk])
        mid = jnp.square(jnp.maximum(mid, 0.0)).astype(BF16)
        acc = acc + _dot(mid, wd_ref[f:f + f_chunk, :])
    if final_norm:
        acc = _rms(acc, nf_ref[...])
    o_ref[...] = acc


def _mlp(x, nw_all, wu_all, wd_all, layer, nf, final_norm, tm, name):
    m, d = x.shape
    return pl.pallas_call(
        functools.partial(_mlp_kernel, final_norm=final_norm, f_chunk=1024),
        grid=(m // tm,),
        in_specs=[pl.BlockSpec((tm, d), lambda i: (i, 0)),
                  _layer_resident((1, d), layer),
                  _layer_resident((d, D_FF), layer),
                  _layer_resident((D_FF, d), layer),
                  _resident((1, d))],
        out_specs=pl.BlockSpec((tm, d), lambda i: (i, 0)),
        out_shape=jax.ShapeDtypeStruct((m, d), F32),
        compiler_params=_params(1),
        name=name,
    )(x, nw_all, wu_all, wd_all, nf)


def _ssd_small_specs(layer):
    return [_layer_resident((SSD_CONV_W, SSD_CONV_DIM), layer),
            _layer_resident((1, SSD_CONV_DIM), layer),
            _layer_resident((1, SSD_DT_PAD), layer),
            _layer_resident((1, SSD_DT_PAD), layer),
            _layer_resident((1, SSD_D_INNER), layer)]


def _ssd_project_jobs(h_tile, nmix_ref, win_ref, us, xext, zs, dts, t_len, n_blk=256):
    cdim = SSD_CONV_DIM
    us[...] = _rms(h_tile, nmix_ref[...]).astype(BF16)

    def job(dst, rows, c_dst, c_src, n):
        def run():
            dst[rows, c_dst:c_dst + n] = _dot(us[...], win_ref[:, c_src:c_src + n])
        return run

    every = slice(None)
    jobs = [job(dts, every, 0, SSD_D_INNER + cdim, SSD_DT_PAD)]
    jobs += [job(xext, slice(SUBLANES, SUBLANES + t_len), c, SSD_D_INNER + c, n_blk) for c in range(0, cdim, n_blk)]
    jobs += [job(zs, every, c, c, n_blk) for c in range(0, SSD_D_INNER, n_blk)]
    return jobs


def _ssd_tile(h_ref, xext, xext_next, zs, dts, cw_ref, cb_ref, dtb_ref, a_ref, drep_ref, nrm_ref, wout_ref,
              o_ref, rows_out, cst_ref, xc, ys, ht, t_len, q_len, side_jobs):
    jobs = list(side_jobs)
    cdim = SSD_CONV_DIM
    off = SUBLANES - (SSD_CONV_W - 1)
    for c0 in range(0, cdim, 512):
        cs = slice(c0, c0 + 512)
        acc = cb_ref[:, cs] + xext[off:off + t_len, cs] * cw_ref[0:1, cs]
        for w in range(1, SSD_CONV_W):
            acc = acc + xext[off + w:off + w + t_len, cs] * cw_ref[w:w + 1, cs]
        xc[:, cs] = _silu(acc)
    cst_ref[...] = xext[t_len + off:t_len + SUBLANES, :]
    xext_next[0:SUBLANES, :] = xext[t_len:t_len + SUBLANES, :]

    dt_all = _softplus(dts[...] + dtb_ref[...])
    a_row = a_ref[...]
    ri = lax.broadcasted_iota(jnp.int32, (q_len, q_len), 0)
    ci = lax.broadcasted_iota(jnp.int32, (q_len, q_len), 1)
    causal = ri >= ci
    tril = jnp.where(causal, 1.0, 0.0).astype(BF16)
    lane = lax.broadcasted_iota(jnp.int32, (1, LANES), 1)
    lo_mask = lane < SSD_HEAD_DIM

    for c in range(t_len // q_len):
        rs = slice(c * q_len, (c + 1) * q_len)
        dtc = dt_all[rs]
        cum = _cumsum_rows(tril, dtc * a_row)
        cum_t = cum.T
        dt_t = dtc.T
        for g in range(SSD_N_GROUPS):
            b_g = xc[rs, SSD_D_INNER + LANES * g:SSD_D_INNER + LANES * (g + 1)]
            c_g = xc[rs, SSD_D_INNER + SSD_GN + LANES * g:SSD_D_INNER + SSD_GN + LANES * (g + 1)]
            cbm = lax.dot_general(c_g.astype(BF16), b_g.astype(BF16), _NT, preferred_element_type=F32)
            b_gt = b_g.T
            for m in range(SSD_N_HEADS // SSD_N_GROUPS // 2):
                p = g * 4 + m
                ps = slice(LANES * p, LANES * (p + 1))
                xp = xc[rs, ps]
                h_old = ht[p]
                w_mats, c_exps, b_ss, x_ss, h_ss, decs = [], [], [], [], [], []
                for s in range(2):
                    hh = 2 * p + s
                    msk = lo_mask if s == 0 else jnp.logical_not(lo_mask)
                    col = cum[:, hh:hh + 1]
                    row = cum_t[hh:hh + 1, :]
                    dt_row = dt_t[hh:hh + 1, :]
                    dec = jnp.exp(jnp.where(causal, col - row, NEG))
                    w_mats.append((cbm * dec * dt_row).astype(BF16))
                    c_exps.append((c_g * jnp.exp(col)).astype(BF16))
                    x_ss.append(jnp.where(msk, xp, 0.0).astype(BF16))
                    h_ss.append(jnp.where(msk, h_old, 0.0).astype(BF16))
                    last = cum[q_len - 1:q_len, hh:hh + 1]
                    b_ss.append((b_gt * (jnp.exp(last - row) * dt_row)).astype(BF16))
                    decs.append(jnp.exp(last))
                x_cat = jnp.concatenate(x_ss, axis=0)
                y_acc = (xp * drep_ref[:, ps]
                         + _dot(jnp.concatenate(w_mats + c_exps, axis=1),
                                jnp.concatenate([x_cat] + h_ss, axis=0)))
                dec_row = jnp.where(lo_mask, decs[0], decs[1])
                ht[p] = h_old * dec_row + _dot(jnp.concatenate(b_ss, axis=1), x_cat)
                ys[rs, ps] = y_acc
                if jobs:
                    jobs.pop(0)()

    while jobs:
        jobs.pop(0)()
    o_ref[rows_out, :] = _gate_out_compute(ys, zs, nrm_ref, wout_ref, h_ref[rows_out, :], SSD_N_GROUPS, True)


def _ssd_prompt_kernel(h_ref, hn_ref, nmix_ref, win_ref, cw_ref, cb_ref, dtb_ref, a_ref, drep_ref, nrm_ref, wout_ref,
                       o_ref, cst_ref, sst_ref, us, xa, xb, za, zb, da, db, xc, ys, ht, *, t_len, q_len):
    i = pl.program_id(1)
    rows_a, rows_b = slice(0, t_len), slice(t_len, 2 * t_len)
    rest = (cw_ref, cb_ref, dtb_ref, a_ref, drep_ref, nrm_ref, wout_ref)

    @pl.when(i == 0)
    def _():
        xa[0:SUBLANES, :] = jnp.zeros((SUBLANES, SSD_CONV_DIM), F32)
        ht[...] = jnp.zeros_like(ht)
        for job in _ssd_project_jobs(h_ref[rows_a, :], nmix_ref, win_ref, us, xa, za, da, t_len):
            job()

    jobs_b = _ssd_project_jobs(h_ref[rows_b, :], nmix_ref, win_ref, us, xb, zb, db, t_len)
    _ssd_tile(h_ref, xa, xb, za, da, *rest, o_ref, rows_a, cst_ref, xc, ys, ht, t_len, q_len, jobs_b)
    jobs_a = _ssd_project_jobs(hn_ref[...], nmix_ref, win_ref, us, xa, za, da, t_len)
    _ssd_tile(h_ref, xb, xa, zb, db, *rest, o_ref, rows_b, cst_ref, xc, ys, ht, t_len, q_len, jobs_a)

    @pl.when(i == pl.num_programs(1) - 1)
    def _():
        for p in range(SSD_N_HEADS // 2):
            sst_ref[LANES * p:LANES * (p + 1), :] = ht[p].T


def _ssd_prompt_layer(h, p, layer, j, batch, seq):
    t_len, q_len = SSD_PROMPT_T, SSD_PROMPT_Q
    assert seq % (2 * t_len) == 0
    ns = seq // (2 * t_len)
    tok = lambda b, i: (b * ns + i, 0)
    nxt = lambda b, i: (2 * (b * ns + jnp.minimum(i + 1, ns - 1)), 0)
    n_in = SSD_D_INNER + SSD_CONV_DIM + SSD_DT_PAD
    xbuf = pltpu.VMEM((t_len + SUBLANES, SSD_CONV_DIM), F32)
    zbuf = pltpu.VMEM((t_len, SSD_D_INNER), F32)
    dbuf = pltpu.VMEM((t_len, SSD_DT_PAD), F32)
    return pl.pallas_call(
        functools.partial(_ssd_prompt_kernel, t_len=t_len, q_len=q_len),
        grid=(batch, ns),
        in_specs=[pl.BlockSpec((2 * t_len, D_MODEL), tok),
                  pl.BlockSpec((t_len, D_MODEL), nxt),
                  _layer_resident((1, D_MODEL), layer),
                  _layer_resident((D_MODEL, n_in), j)] + _ssd_small_specs(j) + [
                  _layer_resident((1, SSD_D_INNER), j),
                  _layer_resident((SSD_D_INNER, D_MODEL), j)],
        out_specs=[pl.BlockSpec((2 * t_len, D_MODEL), tok),
                   pl.BlockSpec((None, SSD_CONV_W - 1, SSD_CONV_DIM), lambda b, i: (b, 0, 0)),
                   pl.BlockSpec((None, SSD_N_HEADS * SSD_HEAD_DIM, SSD_D_STATE), lambda b, i: (b, 0, 0))],
        out_shape=[jax.ShapeDtypeStruct((batch * seq, D_MODEL), F32),
                   jax.ShapeDtypeStruct((batch, SSD_CONV_W - 1, SSD_CONV_DIM), F32),
                   jax.ShapeDtypeStruct((batch, SSD_N_HEADS * SSD_HEAD_DIM, SSD_D_STATE), F32)],
        scratch_shapes=[pltpu.VMEM((t_len, D_MODEL), BF16), xbuf, xbuf, zbuf, zbuf, dbuf, dbuf,
                        pltpu.VMEM((t_len, SSD_CONV_DIM), F32),
                        pltpu.VMEM((t_len, SSD_D_INNER), F32),
                        pltpu.VMEM((SSD_N_HEADS // 2, SSD_D_STATE, LANES), F32)],
        compiler_params=_params(2),
        name="ssd_prompt_layer",
    )(h, h, p["norm_mix"], p["ssd_w_in"], *p["ssd_small"], p["ssd_norm"], p["ssd_w_out"])


def _hgrn_gates(q_raw, fz, lb, lbc):
    e = jnp.exp(-jnp.abs(fz))
    r = 1.0 / (1.0 + e)
    er = e * r
    pos = fz >= 0.0
    sig_pos = jnp.where(pos, r, er)
    sig_neg = jnp.where(pos, er, r)
    logf = jnp.log(sig_pos + lbc * sig_neg)
    return _silu(q_raw), (1.0 - lb) * sig_neg, logf


def _hgrn_project_jobs(h_tile, nmix_ref, win_ref, lb_ref, lbc_ref, us, buf, t_len, c_len, n_blk=256):
    qs, ks, cs, vs, gs, ws = buf
    fsub = HGRN_FAST_SUB
    us[...] = _rms(h_tile, nmix_ref[...]).astype(BF16)
    ri = lax.broadcasted_iota(jnp.int32, (c_len, c_len), 0)
    ci = lax.broadcasted_iota(jnp.int32, (c_len, c_len), 1)
    tril = jnp.where(ri >= ci, 1.0, 0.0).astype(BF16)

    def gate_job(c0):
        def run():
            cols = slice(c0, c0 + n_blk)
            fz = _dot(us[...], win_ref[:, HGRN_F + c0:HGRN_F + c0 + n_blk])
            qr = _dot(us[...], win_ref[:, c0:c0 + n_blk])
            q, k, logf = _hgrn_gates(qr, fz, lb_ref[:, cols], lbc_ref[:, cols])
            qs[:, cols] = q
            ks[:, cols] = k
            worst = None
            for c in range(t_len // c_len):
                cum = _cumsum_rows(tril, logf[c * c_len:(c + 1) * c_len])
                cs[c * c_len:(c + 1) * c_len, cols] = cum
                prev = None
                for a in range(c_len // fsub):
                    end = cum[(a + 1) * fsub - 1:(a + 1) * fsub, :]
                    drop = -end if prev is None else prev - end
                    worst = drop if worst is None else jnp.maximum(worst, drop)
                    prev = end
            w = worst[:, 0:LANES]
            for l0 in range(LANES, n_blk, LANES):
                w = jnp.maximum(w, worst[:, l0:l0 + LANES])
            ws[...] = w if c0 == 0 else jnp.maximum(ws[...], w)
        return run

    def plain_job(dst, c_src, c0):
        def run():
            dst[:, c0:c0 + n_blk] = _dot(us[...], win_ref[:, c_src + c0:c_src + c0 + n_blk])
        return run

    jobs = []
    for c0 in range(0, HGRN_F, n_blk):
        jobs += [gate_job(c0), plain_job(vs, 2 * HGRN_F, c0), plain_job(gs, 2 * HGRN_F + D_MODEL, c0)]
    return jobs


def _hgrn_tile(h_ref, rows, buf, nrm_ref, wout_ref, o_ref, os_, stt, t_len, c_len, side_jobs):
    qs, ks, cs, v_ref, g_ref, ws = buf
    nh = HGRN_N_HEADS
    n_chunks = t_len // c_len
    fsub = HGRN_FAST_SUB
    ri = lax.broadcasted_iota(jnp.int32, (c_len, c_len), 0)
    ci = lax.broadcasted_iota(jnp.int32, (c_len, c_len), 1)
    causal = ri >= ci
    fast = jnp.max(ws[...]) <= HGRN_FAST_RANGE

    row_c = lax.broadcasted_iota(jnp.int32, (c_len, LANES), 0)

    def chunk_head(h, hs, c, att_fn, o_extra):
        rs = slice(c * c_len, (c + 1) * c_len)
        qh, kh, vh, cum = qs[rs, hs], ks[rs, hs], v_ref[rs, hs], cs[rs, hs]
        st_old = stt[h]
        vb16 = vh.astype(BF16)
        o = lax.dot_general((qh * jnp.exp(cum)).astype(BF16), st_old.astype(BF16), _NT,
                            preferred_element_type=F32)
        o = o + _dot(att_fn(qh, kh, cum).astype(BF16), vb16)
        if o_extra is not None:
            o = o + o_extra[rs]
        os_[rs, hs] = o
        last = cum[c_len - 1:c_len, :]
        k_hat = (kh * jnp.exp(last - cum)).astype(BF16)
        upd = lax.dot_general(vb16, k_hat, _TN, preferred_element_type=F32)
        stt[h] = st_old * jnp.exp(last) + upd

    def cross_operands(qh, kh, cum, s):
        ref_rows = jnp.concatenate(
            [jnp.broadcast_to(cum[blk * 2 * s + s - 1:blk * 2 * s + s, :], (2 * s, LANES))
             for blk in range(c_len // (2 * s))], axis=0)
        second = (row_c // s) % 2 == 1
        e = jnp.exp(jnp.where(second, cum - ref_rows, ref_rows - cum))
        q_t = jnp.where(second, qh * e, 0.0).astype(BF16)
        k_t = jnp.where(second, 0.0, kh * e).astype(BF16)
        return q_t, k_t, s

    def cross_level(qh, kh, cum, s):
        q_t, k_t, _ = cross_operands(qh, kh, cum, s)
        part = lax.dot_general(q_t, k_t, _NT, preferred_element_type=F32)
        if 2 * s < c_len:
            part = jnp.where((ri // (2 * s)) == (ci // (2 * s)), part, 0.0)
        return part

    def levels_down_to(sub):
        out = []
        s = c_len // 2
        while s >= sub:
            out.append(s)
            s //= 2
        return out

    @pl.when(fast)
    def _():
        jobs = list(side_jobs)
        same_sub = causal & ((ri // fsub) == (ci // fsub))
        for c in range(n_chunks):
            rs = slice(c * c_len, (c + 1) * c_len)
            for h0 in range(0, nh, 4):
                heads = range(h0, h0 + 4)
                pre = {}
                for h in heads:
                    hs = slice(LANES * h, LANES * (h + 1))
                    qh, kh, cum = qs[rs, hs], ks[rs, hs], cs[rs, hs]
                    start = jnp.concatenate(
                        [jnp.zeros((fsub, LANES), F32)] +
                        [jnp.broadcast_to(cum[a * fsub - 1:a * fsub, :], (fsub, LANES))
                         for a in range(1, c_len // fsub)], axis=0)
                    last = cum[c_len - 1:c_len, :]
                    pre[h] = dict(
                        hs=hs, last=last, st_old=stt[h], vb16=v_ref[rs, hs].astype(BF16),
                        qe=(qh * jnp.exp(cum)).astype(BF16),
                        q_d=(qh * jnp.exp(cum - start)).astype(BF16),
                        k_d=(kh * jnp.exp(start - cum)).astype(BF16),
                        k_hat=(kh * jnp.exp(last - cum)).astype(BF16),
                        cross=[cross_operands(qh, kh, cum, s) for s in levels_down_to(fsub)])
                if jobs:
                    jobs.pop(0)()
                mm = {}
                for h in heads:
                    p = pre[h]
                    mm[h] = dict(
                        o=lax.dot_general(p["qe"], p["st_old"].astype(BF16), _NT, preferred_element_type=F32),
                        att=lax.dot_general(p["q_d"], p["k_d"], _NT, preferred_element_type=F32),
                        cross=[lax.dot_general(q_t, k_t, _NT, preferred_element_type=F32)
                               for (q_t, k_t, _) in p["cross"]],
                        upd=lax.dot_general(p["vb16"], p["k_hat"], _TN, preferred_element_type=F32))
                if jobs:
                    jobs.pop(0)()
                for h in heads:
                    p, m = pre[h], mm[h]
                    att = jnp.where(same_sub, m["att"], 0.0)
                    for part, (_, _, s) in zip(m["cross"], p["cross"]):
                        att = att + (jnp.where((ri // (2 * s)) == (ci // (2 * s)), part, 0.0)
                                     if 2 * s < c_len else part)
                    m["att16"] = att.astype(BF16)
                for h in heads:
                    p, m = pre[h], mm[h]
                    os_[rs, p["hs"]] = m["o"] + _dot(m["att16"], p["vb16"])
                    stt[h] = p["st_old"] * jnp.exp(p["last"]) + m["upd"]
                if jobs:
                    jobs.pop(0)()
        while jobs:
            jobs.pop(0)()

    @pl.when(jnp.logical_not(fast))
    def _():
        for job in side_jobs:
            job()
        sub = HGRN_SUB
        row_in_sub = lax.broadcasted_iota(jnp.int32, (t_len, LANES), 0) % sub

        def att_general(qh, kh, cum):
            att = jnp.zeros((c_len, c_len), F32)
            for s in levels_down_to(sub):
                att = att + cross_level(qh, kh, cum, s)
            return att

        def head_body(h, carry):
            hs = pl.ds(pl.multiple_of(h * LANES, LANES), LANES)
            q_all = qs[:, hs]
            cum_all = cs[:, hs]
            o_diag = jnp.zeros((t_len, LANES), F32)
            for j in range(sub):
                def rows(ref):
                    return jnp.concatenate(
                        [jnp.broadcast_to(ref[sub * a + j:sub * a + j + 1, hs], (sub, LANES))
                         for a in range(t_len // sub)], axis=0)
                k_b, cum_b, v_b = rows(ks), rows(cs), rows(v_ref)
                arg = jnp.where(row_in_sub >= j, cum_all - cum_b, NEG)
                z = q_all * k_b * jnp.exp(arg)
                o_diag = o_diag + jnp.sum(z, axis=-1, keepdims=True) * v_b
            for c in range(n_chunks):
                chunk_head(h, hs, c, att_general, o_diag)
            return carry

        lax.fori_loop(0, nh, head_body, 0)

    o_ref[rows, :] = _gate_out_compute(os_, g_ref, nrm_ref, wout_ref, h_ref[rows, :], nh, False)


def _hgrn_prompt_kernel(h_ref, hn_ref, nmix_ref, win_ref, lb_ref, lbc_ref, nrm_ref, wout_ref, o_ref, st_ref,
                        us, qa, ka, ca, va, ga, wa, qb, kb, cb, vb, gb, wb, os_, stt, *, t_len, c_len):
    i = pl.program_id(1)
    rows_a, rows_b = slice(0, t_len), slice(t_len, 2 * t_len)
    buf_a = (qa, ka, ca, va, ga, wa)
    buf_b = (qb, kb, cb, vb, gb, wb)
    proj = (nmix_ref, win_ref, lb_ref, lbc_ref, us)

    @pl.when(i == 0)
    def _():
        stt[...] = jnp.zeros_like(stt)
        for job in _hgrn_project_jobs(h_ref[rows_a, :], *proj, buf_a, t_len, c_len):
            job()

    jobs_b = _hgrn_project_jobs(h_ref[rows_b, :], *proj, buf_b, t_len, c_len)
    _hgrn_tile(h_ref, rows_a, buf_a, nrm_ref, wout_ref, o_ref, os_, stt, t_len, c_len, jobs_b)
    jobs_a = _hgrn_project_jobs(hn_ref[...], *proj, buf_a, t_len, c_len)
    _hgrn_tile(h_ref, rows_b, buf_b, nrm_ref, wout_ref, o_ref, os_, stt, t_len, c_len, jobs_a)

    @pl.when(i == pl.num_programs(1) - 1)
    def _():
        for h in range(HGRN_N_HEADS):
            st_ref[h] = stt[h].T


def _hgrn_prompt_layer(h, p, layer, j, batch, seq):
    t_len, c_len = HGRN_PROMPT_T, HGRN_PROMPT_C
    assert seq % (2 * t_len) == 0
    ns = seq // (2 * t_len)
    tok = lambda b, i: (b * ns + i, 0)
    nxt = lambda b, i: (2 * (b * ns + jnp.minimum(i + 1, ns - 1)), 0)
    n_in = 2 * HGRN_F + 2 * D_MODEL
    wide = pltpu.VMEM((t_len, HGRN_F), F32)
    worst = pltpu.VMEM((1, LANES), F32)
    return pl.pallas_call(
        functools.partial(_hgrn_prompt_kernel, t_len=t_len, c_len=c_len),
        grid=(batch, ns),
        in_specs=[pl.BlockSpec((2 * t_len, D_MODEL), tok),
                  pl.BlockSpec((t_len, D_MODEL), nxt),
                  _layer_resident((1, D_MODEL), layer),
                  _layer_resident((D_MODEL, n_in), j),
                  _layer_resident((1, HGRN_F), j),
                  _layer_resident((1, HGRN_F), j),
                  _layer_resident((1, D_MODEL), j),
                  _layer_resident((D_MODEL, D_MODEL), j)],
        out_specs=[pl.BlockSpec((2 * t_len, D_MODEL), tok),
                   pl.BlockSpec((None, HGRN_N_HEADS, HGRN_DK, HGRN_DV), lambda b, i: (b, 0, 0, 0))],
        out_shape=[jax.ShapeDtypeStruct((batch * seq, D_MODEL), F32),
                   jax.ShapeDtypeStruct((batch, HGRN_N_HEADS, HGRN_DK, HGRN_DV), F32)],
        scratch_shapes=[pltpu.VMEM((t_len, D_MODEL), BF16)] + [wide] * 5 + [worst] + [wide] * 5 + [worst] + [
                        pltpu.VMEM((t_len, D_MODEL), F32),
                        pltpu.VMEM((HGRN_N_HEADS, HGRN_DV, HGRN_DK), F32)],
        compiler_params=_params(2),
        name="hgrn_prompt_layer",
    )(h, h, p["norm_mix"], p["hgrn_w_in"], p["hgrn_lb"], p["hgrn_lbc"], p["hgrn_norm"], p["hgrn_w_out"])


def _parked(i, l, n_blocks):
    return jnp.where(l == 0, i, n_blocks - 1)


def _state_specs(block, layer, n_layers, prev, n_blocks):
    nd = len(block) - 1
    zeros = (0,) * nd
    in_spec = pl.BlockSpec((None,) + block, lambda l, i: (layer, _parked(i, l, n_blocks)) + zeros)
    if prev is None:
        out_spec = pl.BlockSpec((None,) + block, lambda l, i: (l, i) + zeros)
        return in_spec, out_spec, n_layers
    out_spec = pl.BlockSpec((None,) + block, lambda l, i: (layer, i) + zeros)
    return in_spec, out_spec, 1


def _ssd_sample_body(xbc_ref, dt_ref, cst_ref, sst_ref, cw_ref, cb_ref, dtb_ref, a_ref, drep_ref,
                     e16_ref, e32_ref, y_ref, ncst_ref, nsst_ref, xc, xh, yi, n_tok, bb):
    nw = SSD_CONV_W
    ext = [cst_ref[s] for s in range(nw - 1)] + [xbc_ref[t] for t in range(n_tok)]
    for t in range(n_tok):
        acc = cb_ref[...] + ext[t] * cw_ref[0:1, :]
        for w in range(1, nw):
            acc = acc + ext[t + w] * cw_ref[w:w + 1, :]
        xc[t] = _silu(acc)
    for s in range(nw - 1):
        ncst_ref[s] = ext[n_tok + s]

    a_row = a_ref[...]
    dts, cums = [], []
    run = None
    for t in range(n_tok):
        dt = _softplus(dt_ref[t] + dtb_ref[...])
        run = dt * a_row if run is None else run + dt * a_row
        dts.append(dt)
        cums.append(run)

    lane = lax.broadcasted_iota(jnp.int32, (bb, LANES), 1)
    rpg = SSD_N_HEADS // SSD_N_GROUPS

    def group_dot(t, j):
        out = None
        for g in reversed(range(SSD_N_GROUPS)):
            cg = xc[t, :, SSD_D_INNER + SSD_GN + LANES * g:SSD_D_INNER + SSD_GN + LANES * (g + 1)]
            bg = xc[j, :, SSD_D_INNER + LANES * g:SSD_D_INNER + LANES * (g + 1)]
            sg = jnp.sum(cg * bg, axis=-1, keepdims=True)
            out = jnp.broadcast_to(sg, (bb, LANES)) if out is None else jnp.where(lane < rpg * (g + 1), sg, out)
        return out

    pairs = [(t, j) for t in range(n_tok) for j in range(t + 1)]
    w_rows = [group_dot(t, j) * jnp.exp(cums[t] - cums[j]) * dts[j] for (t, j) in pairs]
    tail_rows = [jnp.exp(cums[n_tok - 1] - cums[j]) * dts[j] for j in range(n_tok)]
    we = _dot(jnp.concatenate(w_rows + tail_rows, axis=0).astype(BF16), e16_ref[...])
    ce = _dot_exact(jnp.concatenate([jnp.exp(cums[t]) for t in range(n_tok)], axis=0), e32_ref[...])

    for j in range(n_tok):
        r0 = (len(pairs) + j) * bb
        xh[j] = we[r0:r0 + bb] * xc[j, :, 0:SSD_D_INNER]

    dec = jnp.exp(cums[n_tok - 1])
    gw = SSD_D_INNER // SSD_N_GROUPS
    for b in range(bb):
        for g in range(SSD_N_GROUPS):
            cb_ = xc[:, b, SSD_D_INNER + SSD_GN + LANES * g:SSD_D_INNER + SSD_GN + LANES * (g + 1)]
            bb_ = xc[:, b, SSD_D_INNER + LANES * g:SSD_D_INNER + LANES * (g + 1)]
            h0 = sst_ref[b, gw * g:gw * (g + 1), :]
            yi[:, b, gw * g:gw * (g + 1)] = lax.dot_general(
                cb_.astype(BF16), h0.astype(BF16), _NT, preferred_element_type=F32)
            xhb = xh[:, b, gw * g:gw * (g + 1)]
            upd = lax.dot_general(xhb.astype(BF16), bb_.astype(BF16), _TN, preferred_element_type=F32)
            for r in range(rpg):
                hh = rpg * g + r
                d = dec[b:b + 1, hh:hh + 1]
                rows = slice(SSD_HEAD_DIM * r, SSD_HEAD_DIM * (r + 1))
                nsst_ref[b, gw * g + SSD_HEAD_DIM * r:gw * g + SSD_HEAD_DIM * (r + 1), :] = h0[rows] * d + upd[rows]

    for t in range(n_tok):
        xt = xc[t, :, 0:SSD_D_INNER]
        y = xt * drep_ref[...] + yi[t] * ce[t * bb:(t + 1) * bb]
        for j in range(t + 1):
            r0 = pairs.index((t, j)) * bb
            y = y + we[r0:r0 + bb] * xc[j, :, 0:SSD_D_INNER]
        y_ref[t] = y


def _ssd_sample_kernel(*refs, n_tok, bb, has_prev):
    if has_prev:
        refs = refs[:11] + refs[12:]
    nsst_ref = refs[13]
    fill = pl.program_id(0)

    @pl.when(fill == 0)
    def _():
        _ssd_sample_body(*refs, n_tok, bb)

    if not has_prev:
        @pl.when(fill > 0)
        def _():
            nsst_ref[...] = jnp.zeros_like(nsst_ref)


def _ssd_sample_scan(xbc, dt_raw, cst, sst_all, prev, small, e16, e32, layer, n_tok, batch):
    bb = SAMPLE_BB
    rows = SSD_N_HEADS * SSD_HEAD_DIM
    n_layers = sst_all.shape[0]
    n_blocks = batch // bb
    tb = lambda l, i: (0, _parked(i, l, n_blocks), 0)
    st_in, st_out, n_fill = _state_specs((bb, rows, SSD_D_STATE), layer, n_layers, prev, n_blocks)
    in_specs = [pl.BlockSpec((n_tok, bb, SSD_CONV_DIM), tb),
                pl.BlockSpec((n_tok, bb, SSD_DT_PAD), tb),
                pl.BlockSpec((SSD_CONV_W - 1, bb, SSD_CONV_DIM), tb),
                st_in] + _ssd_small_specs(layer) + [
                _resident((LANES, SSD_D_INNER)),
                _resident((LANES, SSD_D_INNER))]
    args = [xbc, dt_raw, cst, sst_all, *small, e16, e32]
    aliases = {}
    if prev is not None:
        in_specs.append(pl.BlockSpec(memory_space=pl.ANY))
        args.append(prev)
        aliases = {len(args) - 1: 2}
    return pl.pallas_call(
        functools.partial(_ssd_sample_kernel, n_tok=n_tok, bb=bb, has_prev=prev is not None),
        grid=(n_fill, n_blocks),
        in_specs=in_specs,
        out_specs=[pl.BlockSpec((n_tok, bb, SSD_D_INNER), tb),
                   pl.BlockSpec((SSD_CONV_W - 1, bb, SSD_CONV_DIM), tb),
                   st_out],
        out_shape=[jax.ShapeDtypeStruct((n_tok, batch, SSD_D_INNER), F32),
                   jax.ShapeDtypeStruct((SSD_CONV_W - 1, batch, SSD_CONV_DIM), F32),
                   jax.ShapeDtypeStruct((n_layers, batch, rows, SSD_D_STATE), F32)],
        scratch_shapes=[pltpu.VMEM((n_tok, bb, SSD_CONV_DIM), F32),
                        pltpu.VMEM((n_tok, bb, SSD_D_INNER), F32),
                        pltpu.VMEM((n_tok, bb, SSD_D_INNER), F32)],
        input_output_aliases=aliases,
        compiler_params=_params(2),
        name="ssd_sample_scan",
    )(*args)


def _hgrn_sample_body(q_ref, f_ref, v_ref, st_ref, lb_ref, lbc_ref, o_ref, nst_ref,
                      qe, kh, oi, n_tok, bb):
    nh = HGRN_N_HEADS
    qs, ks, cums = [], [], []
    run = None
    for t in range(n_tok):
        q, k, logf = _hgrn_gates(q_ref[t], f_ref[t], lb_ref[...], lbc_ref[...])
        run = logf if run is None else run + logf
        qs.append(q)
        ks.append(k)
        cums.append(run)
    last = cums[n_tok - 1]
    for t in range(n_tok):
        qe[t] = qs[t] * jnp.exp(cums[t])
        kh[t] = ks[t] * jnp.exp(last - cums[t])
    dec = jnp.exp(last)

    for h in range(nh):
        hs = slice(LANES * h, LANES * (h + 1))
        dec_t = dec[:, hs].T
        for b in range(bb):
            s0 = st_ref[b, h]
            oi[:, b, hs] = _dot(qe[:, b, hs].astype(BF16), s0.astype(BF16))
            upd = lax.dot_general(kh[:, b, hs].astype(BF16), v_ref[:, b, hs].astype(BF16), _TN,
                                  preferred_element_type=F32)
            nst_ref[b, h] = s0 * dec_t[:, b:b + 1] + upd

    for t in range(n_tok):
        o = oi[t]
        for j in range(t + 1):
            z = qs[t] * ks[j] * jnp.exp(cums[t] - cums[j])
            vj = v_ref[j]
            parts = []
            for h in range(nh):
                hs = slice(LANES * h, LANES * (h + 1))
                parts.append(jnp.sum(z[:, hs], axis=-1, keepdims=True) * vj[:, hs])
            o = o + jnp.concatenate(parts, axis=-1)
        o_ref[t] = o


def _hgrn_sample_kernel(*refs, n_tok, bb, has_prev):
    if has_prev:
        refs = refs[:6] + refs[7:]
    nst_ref = refs[7]
    fill = pl.program_id(0)

    @pl.when(fill == 0)
    def _():
        _hgrn_sample_body(*refs, n_tok, bb)

    if not has_prev:
        @pl.when(fill > 0)
        def _():
            nst_ref[...] = jnp.zeros_like(nst_ref)


def _hgrn_sample_scan(qr, fz, v, st_all, prev, lb_all, lbc_all, layer, n_tok, batch):
    bb = SAMPLE_BB
    n_layers = st_all.shape[0]
    n_blocks = batch // bb
    tb = lambda l, i: (0, _parked(i, l, n_blocks), 0)
    st_in, st_out, n_fill = _state_specs((bb, HGRN_N_HEADS, HGRN_DK, HGRN_DV), layer, n_layers, prev, n_blocks)
    in_specs = [pl.BlockSpec((n_tok, bb, HGRN_F), tb),
                pl.BlockSpec((n_tok, bb, HGRN_F), tb),
                pl.BlockSpec((n_tok, bb, D_MODEL), tb),
                st_in,
                _layer_resident((1, HGRN_F), layer),
                _layer_resident((1, HGRN_F), layer)]
    args = [qr, fz, v, st_all, lb_all, lbc_all]
    aliases = {}
    if prev is not None:
        in_specs.append(pl.BlockSpec(memory_space=pl.ANY))
        args.append(prev)
        aliases = {len(args) - 1: 1}
    return pl.pallas_call(
        functools.partial(_hgrn_sample_kernel, n_tok=n_tok, bb=bb, has_prev=prev is not None),
        grid=(n_fill, n_blocks),
        in_specs=in_specs,
        out_specs=[pl.BlockSpec((n_tok, bb, D_MODEL), tb), st_out],
        out_shape=[jax.ShapeDtypeStruct((n_tok, batch, D_MODEL), F32),
                   jax.ShapeDtypeStruct((n_layers, batch, HGRN_N_HEADS, HGRN_DK, HGRN_DV), F32)],
        scratch_shapes=[pltpu.VMEM((n_tok, bb, HGRN_F), F32),
                        pltpu.VMEM((n_tok, bb, HGRN_F), F32),
                        pltpu.VMEM((n_tok, bb, D_MODEL), F32)],
        input_output_aliases=aliases,
        compiler_params=_params(2),
        name="hgrn_sample_scan",
    )(*args)


def _prep_params(norm_mix_w, norm_mlp_w, norm_f_w, ssd_w_in, ssd_conv_w, ssd_conv_b, ssd_dt_bias,
                 ssd_a_log, ssd_d, ssd_norm_w, ssd_w_out, hgrn_w_in, hgrn_lb_raw, hgrn_norm_w,
                 hgrn_w_out, mlp_w_up, mlp_w_down):
    la = ssd_w_in.shape[0]
    pad_dt = SSD_DT_PAD - SSD_N_HEADS
    p = {}
    p["norm_mix"] = norm_mix_w.reshape(DEPTH, 1, D_MODEL)
    p["norm_mlp"] = norm_mlp_w.reshape(DEPTH, 1, D_MODEL)
    p["norm_f"] = norm_f_w.reshape(1, D_MODEL)
    p["ssd_w_in"] = jnp.pad(ssd_w_in, ((0, 0), (0, 0), (0, pad_dt))).astype(BF16)
    p["ssd_small"] = (
        ssd_conv_w,
        ssd_conv_b.reshape(la, 1, SSD_CONV_DIM),
        jnp.pad(ssd_dt_bias, ((0, 0), (0, pad_dt))).reshape(la, 1, SSD_DT_PAD),
        jnp.pad(-jnp.exp(ssd_a_log.astype(F32)), ((0, 0), (0, pad_dt))).reshape(la, 1, SSD_DT_PAD),
        jnp.repeat(ssd_d.astype(F32), SSD_HEAD_DIM, axis=1).reshape(la, 1, SSD_D_INNER))
    p["ssd_norm"] = ssd_norm_w.reshape(la, 1, SSD_D_INNER)
    p["ssd_w_out"] = ssd_w_out.astype(BF16)
    p["hgrn_w_in"] = hgrn_w_in.astype(BF16)
    sm = jax.nn.softmax(hgrn_lb_raw.astype(F32), axis=0)
    lb = jnp.cumsum(sm, axis=0) - sm[0]
    lbn = lb.shape[0]
    p["hgrn_lb"] = lb.reshape(lbn, 1, HGRN_F)
    p["hgrn_lbc"] = jnp.maximum(lb, LB_FLOOR).reshape(lbn, 1, HGRN_F)
    p["hgrn_norm"] = hgrn_norm_w.reshape(lbn, 1, D_MODEL)
    p["hgrn_w_out"] = hgrn_w_out.astype(BF16)
    p["mlp_w_up"] = mlp_w_up.astype(BF16)
    p["mlp_w_down"] = mlp_w_down.astype(BF16)
    head_of_lane = jnp.arange(SSD_D_INNER, dtype=jnp.int32) // SSD_HEAD_DIM
    expand = (jnp.arange(LANES, dtype=jnp.int32)[:, None] == head_of_lane[None, :])
    p["expand16"] = expand.astype(BF16)
    p["expand32"] = expand.astype(F32)
    return p


def _trunk(h, p, prompt, batch, seq, conv_states, ssm_states, hgrn_states):
    tm = min(TOKEN_TILE, h.shape[0])
    new_conv, new_ssm, new_hgrn = [], [], []
    ssm_all, hgrn_all = None, None
    for layer in range(DEPTH):
        j = layer // 2
        if layer % 2 == 0 and prompt:
            h, cst, sst = _ssd_prompt_layer(h, p, layer, j, batch, seq)
            new_conv.append(cst)
            new_ssm.append(sst.reshape(batch, SSD_N_HEADS, SSD_HEAD_DIM, SSD_D_STATE))
        elif layer % 2 == 0:
            z, xbc, dt_raw = _norm_matmul(h, p["norm_mix"], layer, p["ssd_w_in"], j,
                                          (SSD_D_INNER, SSD_CONV_DIM, SSD_DT_PAD), tm, "ssd_in_proj")
            n_l = ssm_states.shape[0]
            y, cst, ssm_all = _ssd_sample_scan(
                xbc.reshape(seq, batch, SSD_CONV_DIM), dt_raw.reshape(seq, batch, SSD_DT_PAD),
                jnp.swapaxes(conv_states[j], 0, 1),
                ssm_states.reshape(n_l, batch, SSD_N_HEADS * SSD_HEAD_DIM, SSD_D_STATE), ssm_all,
                p["ssd_small"], p["expand16"], p["expand32"], j, seq, batch)
            new_conv.append(jnp.swapaxes(cst, 0, 1))
            h = _gate_out(y.reshape(seq * batch, SSD_D_INNER), z, p["ssd_norm"], p["ssd_w_out"], j, h,
                          SSD_N_GROUPS, True, tm, "ssd_out_proj")
        elif prompt:
            h, st = _hgrn_prompt_layer(h, p, layer, j, batch, seq)
            new_hgrn.append(st)
        else:
            qr, fz, v, g = _norm_matmul(h, p["norm_mix"], layer, p["hgrn_w_in"], j,
                                        (HGRN_F, HGRN_F, D_MODEL, D_MODEL), tm, "hgrn_in_proj")
            o, hgrn_all = _hgrn_sample_scan(
                qr.reshape(seq, batch, HGRN_F), fz.reshape(seq, batch, HGRN_F),
                v.reshape(seq, batch, D_MODEL), hgrn_states, hgrn_all,
                p["hgrn_lb"], p["hgrn_lbc"], j, seq, batch)
            h = _gate_out(o.reshape(seq * batch, D_MODEL), g, p["hgrn_norm"], p["hgrn_w_out"], j, h,
                          HGRN_N_HEADS, False, tm, "hgrn_out_proj")
        h = _mlp(h, p["norm_mlp"], p["mlp_w_up"], p["mlp_w_down"], layer, p["norm_f"],
                 layer == DEPTH - 1, tm, "mlp")
    if prompt:
        return h, jnp.stack(new_conv), jnp.stack(new_ssm), jnp.stack(new_hgrn)
    n_l = ssm_states.shape[0]
    return (h, jnp.stack(new_conv),
            ssm_all.reshape(n_l, batch, SSD_N_HEADS, SSD_HEAD_DIM, SSD_D_STATE), hgrn_all)


def kernel(x_prompt, x_sample, state_ssd_conv, state_ssd_ssm, state_hgrn, norm_mix_w, norm_mlp_w, norm_f_w,
           ssd_w_in, ssd_conv_w, ssd_conv_b, ssd_dt_bias, ssd_a_log, ssd_d, ssd_norm_w, ssd_w_out,
           hgrn_w_in, hgrn_lb_raw, hgrn_norm_w, hgrn_w_out, mlp_w_up, mlp_w_down):
    p = _prep_params(norm_mix_w, norm_mlp_w, norm_f_w, ssd_w_in, ssd_conv_w, ssd_conv_b, ssd_dt_bias,
                     ssd_a_log, ssd_d, ssd_norm_w, ssd_w_out, hgrn_w_in, hgrn_lb_raw, hgrn_norm_w,
                     hgrn_w_out, mlp_w_up, mlp_w_down)
    bp, lp, d = x_prompt.shape
    bs, ls, _ = x_sample.shape
    y_p, conv_p, ssm_p, hgrn_p = _trunk(x_prompt.reshape(bp * lp, d), p, True, bp, lp, None, None, None)
    xs = jnp.swapaxes(x_sample, 0, 1).reshape(ls * bs, d)
    y_s, conv_s, ssm_s, hgrn_s = _trunk(xs, p, False, bs, ls, state_ssd_conv, state_ssd_ssm, state_hgrn)
    y_s = jnp.swapaxes(y_s.reshape(ls, bs, d), 0, 1)
    return (y_p.reshape(bp, lp, d), y_s, conv_p, ssm_p, hgrn_p, conv_s, ssm_s, hgrn_s)
```

```python
import functools

import jax
import jax.numpy as jnp
from jax import lax
from jax.experimental import pallas as pl
from jax.experimental.pallas import tpu as pltpu

F32 = jnp.float32
BF16 = jnp.bfloat16

D_MODEL = 1024
DEPTH = 4
SSD_D_INNER = 2048
SSD_HEAD_DIM = 64
SSD_N_HEADS = 32
SSD_N_GROUPS = 4
SSD_D_STATE = 128
SSD_CONV_W = 4
SSD_GN = SSD_N_GROUPS * SSD_D_STATE
SSD_CONV_DIM = SSD_D_INNER + 2 * SSD_GN
SSD_DT_PAD = 128
HGRN_N_HEADS = 8
HGRN_DK = 128
HGRN_DV = 128
HGRN_F = HGRN_N_HEADS * HGRN_DK
D_FF = 4 * D_MODEL
EPS = 1e-5
LB_FLOOR = 1e-20
NEG = -1e30
LANES = 128
SUBLANES = 8
VMEM_LIMIT = 56 * 1024 * 1024

TOKEN_TILE = 512
SSD_PROMPT_T = 256
SSD_PROMPT_Q = 128
HGRN_PROMPT_T = 256
HGRN_PROMPT_C = 64
HGRN_SUB = 8
HGRN_FAST_SUB = 32
HGRN_FAST_RANGE = 75.0
SAMPLE_BB = 8

_NT = (((1,), (1,)), ((), ()))
_TN = (((0,), (0,)), ((), ()))


def _sigmoid(x):
    return jax.nn.sigmoid(x)


def _silu(x):
    return x * jax.nn.sigmoid(x)


def _softplus(x):
    return jnp.maximum(x, 0.0) + jnp.log1p(jnp.exp(-jnp.abs(x)))


def _rms(x, w):
    ms = jnp.mean(x * x, axis=-1, keepdims=True)
    return x * lax.rsqrt(ms + EPS) * w


def _dot(a, b):
    return jnp.dot(a, b, preferred_element_type=F32)


def _dot_exact(a, b):
    return jnp.dot(a, b, precision=lax.Precision.HIGHEST, preferred_element_type=F32)


def _cumsum_rows(tril, x):
    hi = x.astype(BF16)
    r1 = x - hi.astype(F32)
    mid = r1.astype(BF16)
    lo = (r1 - mid.astype(F32)).astype(BF16)
    return _dot(tril, hi) + _dot(tril, mid) + _dot(tril, lo)


def _params(n_axes):
    return pltpu.CompilerParams(dimension_semantics=("arbitrary",) * n_axes,
                                vmem_limit_bytes=VMEM_LIMIT)


def _resident(shape):
    nd = len(shape)
    return pl.BlockSpec(shape, lambda *_: (0,) * nd, pipeline_mode=pl.Buffered(1))


def _layer_resident(shape, layer):
    nd = len(shape)
    return pl.BlockSpec((None,) + tuple(shape), lambda *_: (layer,) + (0,) * nd,
                        pipeline_mode=pl.Buffered(1))


def _norm_matmul_kernel(x_ref, nw_ref, w_ref, *o_refs):
    u = _rms(x_ref[...], nw_ref[...]).astype(BF16)
    c = 0
    for o_ref in o_refs:
        n = o_ref.shape[-1]
        o_ref[...] = _dot(u, w_ref[:, c:c + n])
        c += n


def _norm_matmul(x, nw_all, nw_layer, w_all, w_layer, splits, tm, name):
    m, d = x.shape
    n_tot = w_all.shape[-1]
    assert sum(splits) == n_tot and m % tm == 0
    return pl.pallas_call(
        _norm_matmul_kernel,
        grid=(m // tm,),
        in_specs=[pl.BlockSpec((tm, d), lambda i: (i, 0)),
                  _layer_resident((1, d), nw_layer),
                  _layer_resident((d, n_tot), w_layer)],
        out_specs=[pl.BlockSpec((tm, n), lambda i: (i, 0)) for n in splits],
        out_shape=[jax.ShapeDtypeStruct((m, n), F32) for n in splits],
        compiler_params=_params(1),
        name=name,
    )(x, nw_all, w_all)


def _gate_out_compute(y_ref, g_ref, nw_ref, w_ref, acc, n_groups, gate_first):
    gs = y_ref.shape[-1] // n_groups
    for k in range(n_groups):
        sl = slice(k * gs, (k + 1) * gs)
        yk = y_ref[:, sl]
        sg = _silu(g_ref[:, sl])
        if gate_first:
            yk = yk * sg
        ms = jnp.mean(yk * yk, axis=-1, keepdims=True)
        yn = yk * lax.rsqrt(ms + EPS) * nw_ref[:, sl]
        if not gate_first:
            yn = yn * sg
        acc = acc + _dot(yn.astype(BF16), w_ref[sl, :])
    return acc


def _gate_out_kernel(y_ref, g_ref, nw_ref, w_ref, h_ref, o_ref, *, n_groups, gate_first):
    o_ref[...] = _gate_out_compute(y_ref, g_ref, nw_ref, w_ref, h_ref[...], n_groups, gate_first)


def _gate_out(y, g, nw_all, w_all, layer, h, n_groups, gate_first, tm, name):
    m, di = y.shape
    d = h.shape[1]
    return pl.pallas_call(
        functools.partial(_gate_out_kernel, n_groups=n_groups, gate_first=gate_first),
        grid=(m // tm,),
        in_specs=[pl.BlockSpec((tm, di), lambda i: (i, 0)),
                  pl.BlockSpec((tm, di), lambda i: (i, 0)),
                  _layer_resident((1, di), layer),
                  _layer_resident((di, d), layer),
                  pl.BlockSpec((tm, d), lambda i: (i, 0))],
        out_specs=pl.BlockSpec((tm, d), lambda i: (i, 0)),
        out_shape=jax.ShapeDtypeStruct((m, d), F32),
        compiler_params=_params(1),
        name=name,
    )(y, g, nw_all, w_all, h)


def _mlp_kernel(x_ref, nw_ref, wu_ref, wd_ref, nf_ref, o_ref, *, final_norm, f_chunk):
    x = x_ref[...]
    u = _rms(x, nw_ref[...]).astype(BF16)
    acc = x
    for f in range(0, D_FF, f_chunk):
        mid = _dot(u, wu_ref[:, f:f + f_chunk])
        mid = jnp.square(jnp.maximum(mid, 0.0)).astype(BF16)
        acc = acc + _dot(mid, wd_ref[f:f + f_chunk, :])
    if final_norm:
        acc = _rms(acc, nf_ref[...])
    o_ref[...] = acc


def _mlp(x, nw_all, wu_all, wd_all, layer, nf, final_norm, tm, name):
    m, d = x.shape
    return pl.pallas_call(
        functools.partial(_mlp_kernel, final_norm=final_norm, f_chunk=1024),
        grid=(m // tm,),
        in_specs=[pl.BlockSpec((tm, d), lambda i: (i, 0)),
                  _layer_resident((1, d), layer),
                  _layer_resident((d, D_FF), layer),
                  _layer_resident((D_FF, d), layer),
                  _resident((1, d))],
        out_specs=pl.BlockSpec((tm, d), lambda i: (i, 0)),
        out_shape=jax.ShapeDtypeStruct((m, d), F32),
        compiler_params=_params(1),
        name=name,
    )(x, nw_all, wu_all, wd_all, nf)


def _ssd_small_specs(layer):
    return [_layer_resident((SSD_CONV_W, SSD_CONV_DIM), layer),
            _layer_resident((1, SSD_CONV_DIM), layer),
            _layer_resident((1, SSD_DT_PAD), layer),
            _layer_resident((1, SSD_DT_PAD), layer),
            _layer_resident((1, SSD_D_INNER), layer)]


def _ssd_project_jobs(h_tile, nmix_ref, win_ref, us, xext, zs, dts, t_len, n_blk=256):
    cdim = SSD_CONV_DIM
    us[...] = _rms(h_tile, nmix_ref[...]).astype(BF16)

    def job(dst, rows, c_dst, c_src, n):
        def run():
            dst[rows, c_dst:c_dst + n] = _dot(us[...], win_ref[:, c_src:c_src + n])
        return run

    every = slice(None)
    jobs = [job(dts, every, 0, SSD_D_INNER + cdim, SSD_DT_PAD)]
    jobs += [job(xext, slice(SUBLANES, SUBLANES + t_len), c, SSD_D_INNER + c, n_blk) for c in range(0, cdim, n_blk)]
    jobs += [job(zs, every, c, c, n_blk) for c in range(0, SSD_D_INNER, n_blk)]
    return jobs


def _ssd_tile(h_ref, xext, xext_next, zs, dts, cw_ref, cb_ref, dtb_ref, a_ref, drep_ref, nrm_ref, wout_ref,
              o_ref, rows_out, cst_ref, xc, ys, ht, t_len, q_len, side_jobs):
    jobs = list(side_jobs)
    cdim = SSD_CONV_DIM
    off = SUBLANES - (SSD_CONV_W - 1)
    for c0 in range(0, cdim, 512):
        cs = slice(c0, c0 + 512)
        acc = cb_ref[:, cs] + xext[off:off + t_len, cs] * cw_ref[0:1, cs]
        for w in range(1, SSD_CONV_W):
            acc = acc + xext[off + w:off + w + t_len, cs] * cw_ref[w:w + 1, cs]
        xc[:, cs] = _silu(acc)
    cst_ref[...] = xext[t_len + off:t_len + SUBLANES, :]
    xext_next[0:SUBLANES, :] = xext[t_len:t_len + SUBLANES, :]

    dt_all = _softplus(dts[...] + dtb_ref[...])
    a_row = a_ref[...]
    ri = lax.broadcasted_iota(jnp.int32, (q_len, q_len), 0)
    ci = lax.broadcasted_iota(jnp.int32, (q_len, q_len), 1)
    causal = ri >= ci
    tril = jnp.where(causal, 1.0, 0.0).astype(BF16)
    lane = lax.broadcasted_iota(jnp.int32, (1, LANES), 1)
    lo_mask = lane < SSD_HEAD_DIM

    for c in range(t_len // q_len):
        rs = slice(c * q_len, (c + 1) * q_len)
        dtc = dt_all[rs]
        cum = _cumsum_rows(tril, dtc * a_row)
        cum_t = cum.T
        dt_t = dtc.T
        for g in range(SSD_N_GROUPS):
            b_g = xc[rs, SSD_D_INNER + LANES * g:SSD_D_INNER + LANES * (g + 1)]
            c_g = xc[rs, SSD_D_INNER + SSD_GN + LANES * g:SSD_D_INNER + SSD_GN + LANES * (g + 1)]
            cbm = lax.dot_general(c_g.astype(BF16), b_g.astype(BF16), _NT, preferred_element_type=F32)
            b_gt = b_g.T
            for m in range(SSD_N_HEADS // SSD_N_GROUPS // 2):
                p = g * 4 + m
                ps = slice(LANES * p, LANES * (p + 1))
                xp = xc[rs, ps]
                h_old = ht[p]
                w_mats, c_exps, b_ss, x_ss, h_ss, decs = [], [], [], [], [], []
                for s in range(2):
                    hh = 2 * p + s
                    msk = lo_mask if s == 0 else jnp.logical_not(lo_mask)
                    col = cum[:, hh:hh + 1]
                    row = cum_t[hh:hh + 1, :]
                    dt_row = dt_t[hh:hh + 1, :]
                    dec = jnp.exp(jnp.where(causal, col - row, NEG))
                    w_mats.append((cbm * dec * dt_row).astype(BF16))
                    c_exps.append((c_g * jnp.exp(col)).astype(BF16))
                    x_ss.append(jnp.where(msk, xp, 0.0).astype(BF16))
                    h_ss.append(jnp.where(msk, h_old, 0.0).astype(BF16))
                    last = cum[q_len - 1:q_len, hh:hh + 1]
                    b_ss.append((b_gt * (jnp.exp(last - row) * dt_row)).astype(BF16))
                    decs.append(jnp.exp(last))
                x_cat = jnp.concatenate(x_ss, axis=0)
                y_acc = (xp * drep_ref[:, ps]
                         + _dot(jnp.concatenate(w_mats + c_exps, axis=1),
                                jnp.concatenate([x_cat] + h_ss, axis=0)))
                dec_row = jnp.where(lo_mask, decs[0], decs[1])
                ht[p] = h_old * dec_row + _dot(jnp.concatenate(b_ss, axis=1), x_cat)
                ys[rs, ps] = y_acc
                if jobs:
                    jobs.pop(0)()

    while jobs:
        jobs.pop(0)()
    o_ref[rows_out, :] = _gate_out_compute(ys, zs, nrm_ref, wout_ref, h_ref[rows_out, :], SSD_N_GROUPS, True)


def _ssd_prompt_kernel(h_ref, hn_ref, nmix_ref, win_ref, cw_ref, cb_ref, dtb_ref, a_ref, drep_ref, nrm_ref, wout_ref,
                       o_ref, cst_ref, sst_ref, us, xa, xb, za, zb, da, db, xc, ys, ht, *, t_len, q_len):
    i = pl.program_id(1)
    rows_a, rows_b = slice(0, t_len), slice(t_len, 2 * t_len)
    rest = (cw_ref, cb_ref, dtb_ref, a_ref, drep_ref, nrm_ref, wout_ref)

    @pl.when((i == 0) & (pl.program_id(0) == 0))
    def _():
        for job in _ssd_project_jobs(h_ref[rows_a, :], nmix_ref, win_ref, us, xa, za, da, t_len):
            job()

    @pl.when(i == 0)
    def _():
        xa[0:SUBLANES, :] = jnp.zeros((SUBLANES, SSD_CONV_DIM), F32)
        ht[...] = jnp.zeros_like(ht)

    jobs_b = _ssd_project_jobs(h_ref[rows_b, :], nmix_ref, win_ref, us, xb, zb, db, t_len)
    _ssd_tile(h_ref, xa, xb, za, da, *rest, o_ref, rows_a, cst_ref, xc, ys, ht, t_len, q_len, jobs_b)
    jobs_a = _ssd_project_jobs(hn_ref[...], nmix_ref, win_ref, us, xa, za, da, t_len)
    _ssd_tile(h_ref, xb, xa, zb, db, *rest, o_ref, rows_b, cst_ref, xc, ys, ht, t_len, q_len, jobs_a)

    @pl.when(i == pl.num_programs(1) - 1)
    def _():
        for p in range(SSD_N_HEADS // 2):
            sst_ref[LANES * p:LANES * (p + 1), :] = ht[p].T


def _ssd_prompt_layer(h, p, layer, j, batch, seq):
    t_len, q_len = SSD_PROMPT_T, SSD_PROMPT_Q
    assert seq % (2 * t_len) == 0
    ns = seq // (2 * t_len)
    tok = lambda b, i: (b * ns + i, 0)
    nxt = lambda b, i: (2 * jnp.minimum(b * ns + i + 1, batch * ns - 1), 0)
    n_in = SSD_D_INNER + SSD_CONV_DIM + SSD_DT_PAD
    xbuf = pltpu.VMEM((t_len + SUBLANES, SSD_CONV_DIM), F32)
    zbuf = pltpu.VMEM((t_len, SSD_D_INNER), F32)
    dbuf = pltpu.VMEM((t_len, SSD_DT_PAD), F32)
    return pl.pallas_call(
        functools.partial(_ssd_prompt_kernel, t_len=t_len, q_len=q_len),
        grid=(batch, ns),
        in_specs=[pl.BlockSpec((2 * t_len, D_MODEL), tok),
                  pl.BlockSpec((t_len, D_MODEL), nxt),
                  _layer_resident((1, D_MODEL), layer),
                  _layer_resident((D_MODEL, n_in), j)] + _ssd_small_specs(j) + [
                  _layer_resident((1, SSD_D_INNER), j),
                  _layer_resident((SSD_D_INNER, D_MODEL), j)],
        out_specs=[pl.BlockSpec((2 * t_len, D_MODEL), tok),
                   pl.BlockSpec((None, SSD_CONV_W - 1, SSD_CONV_DIM), lambda b, i: (b, 0, 0)),
                   pl.BlockSpec((None, SSD_N_HEADS * SSD_HEAD_DIM, SSD_D_STATE), lambda b, i: (b, 0, 0))],
        out_shape=[jax.ShapeDtypeStruct((batch * seq, D_MODEL), F32),
                   jax.ShapeDtypeStruct((batch, SSD_CONV_W - 1, SSD_CONV_DIM), F32),
                   jax.ShapeDtypeStruct((batch, SSD_N_HEADS * SSD_HEAD_DIM, SSD_D_STATE), F32)],
        scratch_shapes=[pltpu.VMEM((t_len, D_MODEL), BF16), xbuf, xbuf, zbuf, zbuf, dbuf, dbuf,
                        pltpu.VMEM((t_len, SSD_CONV_DIM), F32),
                        pltpu.VMEM((t_len, SSD_D_INNER), F32),
                        pltpu.VMEM((SSD_N_HEADS // 2, SSD_D_STATE, LANES), F32)],
        compiler_params=_params(2),
        name="ssd_prompt_layer",
    )(h, h, p["norm_mix"], p["ssd_w_in"], *p["ssd_small"], p["ssd_norm"], p["ssd_w_out"])


def _hgrn_gates(q_raw, fz, lb, lbc):
    e = jnp.exp(-jnp.abs(fz))
    r = 1.0 / (1.0 + e)
    er = e * r
    pos = fz >= 0.0
    sig_pos = jnp.where(pos, r, er)
    sig_neg = jnp.where(pos, er, r)
    logf = jnp.log(sig_pos + lbc * sig_neg)
    return _silu(q_raw), (1.0 - lb) * sig_neg, logf


def _hgrn_project_jobs(h_tile, nmix_ref, win_ref, lb_ref, lbc_ref, us, buf, t_len, c_len, n_blk=256):
    qs, ks, cs, vs, gs, ws = buf
    fsub = HGRN_FAST_SUB
    us[...] = _rms(h_tile, nmix_ref[...]).astype(BF16)
    ri = lax.broadcasted_iota(jnp.int32, (c_len, c_len), 0)
    ci = lax.broadcasted_iota(jnp.int32, (c_len, c_len), 1)
    tril = jnp.where(ri >= ci, 1.0, 0.0).astype(BF16)

    def gate_job(c0):
        def run():
            cols = slice(c0, c0 + n_blk)
            fz = _dot(us[...], win_ref[:, HGRN_F + c0:HGRN_F + c0 + n_blk])
            qr = _dot(us[...], win_ref[:, c0:c0 + n_blk])
            q, k, logf = _hgrn_gates(qr, fz, lb_ref[:, cols], lbc_ref[:, cols])
            qs[:, cols] = q
            ks[:, cols] = k
            worst = None
            for c in range(t_len // c_len):
                cum = _cumsum_rows(tril, logf[c * c_len:(c + 1) * c_len])
                cs[c * c_len:(c + 1) * c_len, cols] = cum
                prev = None
                for a in range(c_len // fsub):
                    end = cum[(a + 1) * fsub - 1:(a + 1) * fsub, :]
                    drop = -end if prev is None else prev - end
                    worst = drop if worst is None else jnp.maximum(worst, drop)
                    prev = end
            w = worst[:, 0:LANES]
            for l0 in range(LANES, n_blk, LANES):
                w = jnp.maximum(w, worst[:, l0:l0 + LANES])
            ws[...] = w if c0 == 0 else jnp.maximum(ws[...], w)
        return run

    def plain_job(dst, c_src, c0):
        def run():
            dst[:, c0:c0 + n_blk] = _dot(us[...], win_ref[:, c_src + c0:c_src + c0 + n_blk])
        return run

    jobs = []
    for c0 in range(0, HGRN_F, n_blk):
        jobs += [gate_job(c0), plain_job(vs, 2 * HGRN_F, c0), plain_job(gs, 2 * HGRN_F + D_MODEL, c0)]
    return jobs


def _hgrn_tile(h_ref, rows, buf, nrm_ref, wout_ref, o_ref, os_, stt, t_len, c_len, side_jobs):
    qs, ks, cs, v_ref, g_ref, ws = buf
    nh = HGRN_N_HEADS
    n_chunks = t_len // c_len
    fsub = HGRN_FAST_SUB
    ri = lax.broadcasted_iota(jnp.int32, (c_len, c_len), 0)
    ci = lax.broadcasted_iota(jnp.int32, (c_len, c_len), 1)
    causal = ri >= ci
    fast = jnp.max(ws[...]) <= HGRN_FAST_RANGE

    row_c = lax.broadcasted_iota(jnp.int32, (c_len, LANES), 0)

    def chunk_head(h, hs, c, att_fn, o_extra):
        rs = slice(c * c_len, (c + 1) * c_len)
        qh, kh, vh, cum = qs[rs, hs], ks[rs, hs], v_ref[rs, hs], cs[rs, hs]
        st_old = stt[h]
        vb16 = vh.astype(BF16)
        o = lax.dot_general((qh * jnp.exp(cum)).astype(BF16), st_old.astype(BF16), _NT,
                            preferred_element_type=F32)
        o = o + _dot(att_fn(qh, kh, cum).astype(BF16), vb16)
        if o_extra is not None:
            o = o + o_extra[rs]
        os_[rs, hs] = o
        last = cum[c_len - 1:c_len, :]
        k_hat = (kh * jnp.exp(last - cum)).astype(BF16)
        upd = lax.dot_general(vb16, k_hat, _TN, preferred_element_type=F32)
        stt[h] = st_old * jnp.exp(last) + upd

    def cross_operands(qh, kh, cum, s):
        ref_rows = jnp.concatenate(
            [jnp.broadcast_to(cum[blk * 2 * s + s - 1:blk * 2 * s + s, :], (2 * s, LANES))
             for blk in range(c_len // (2 * s))], axis=0)
        second = (row_c // s) % 2 == 1
        e = jnp.exp(jnp.where(second, cum - ref_rows, ref_rows - cum))
        q_t = jnp.where(second, qh * e, 0.0).astype(BF16)
        k_t = jnp.where(second, 0.0, kh * e).astype(BF16)
        return q_t, k_t, s

    def cross_level(qh, kh, cum, s):
        q_t, k_t, _ = cross_operands(qh, kh, cum, s)
        part = lax.dot_general(q_t, k_t, _NT, preferred_element_type=F32)
        if 2 * s < c_len:
            part = jnp.where((ri // (2 * s)) == (ci // (2 * s)), part, 0.0)
        return part

    def levels_down_to(sub):
        out = []
        s = c_len // 2
        while s >= sub:
            out.append(s)
            s //= 2
        return out

    @pl.when(fast)
    def _():
        jobs = list(side_jobs)
        same_sub = causal & ((ri // fsub) == (ci // fsub))
        for c in range(n_chunks):
            rs = slice(c * c_len, (c + 1) * c_len)
            for h0 in range(0, nh, 4):
                heads = range(h0, h0 + 4)
                pre = {}
                for h in heads:
                    hs = slice(LANES * h, LANES * (h + 1))
                    qh, kh, cum = qs[rs, hs], ks[rs, hs], cs[rs, hs]
                    start = jnp.concatenate(
                        [jnp.zeros((fsub, LANES), F32)] +
                        [jnp.broadcast_to(cum[a * fsub - 1:a * fsub, :], (fsub, LANES))
                         for a in range(1, c_len // fsub)], axis=0)
                    last = cum[c_len - 1:c_len, :]
                    pre[h] = dict(
                        hs=hs, last=last, st_old=stt[h], vb16=v_ref[rs, hs].astype(BF16),
                        qe=(qh * jnp.exp(cum)).astype(BF16),
                        q_d=(qh * jnp.exp(cum - start)).astype(BF16),
                        k_d=(kh * jnp.exp(start - cum)).astype(BF16),
                        k_hat=(kh * jnp.exp(last - cum)).astype(BF16),
                        cross=[cross_operands(qh, kh, cum, s) for s in levels_down_to(fsub)])
                if jobs:
                    jobs.pop(0)()
                mm = {}
                for h in heads:
                    p = pre[h]
                    mm[h] = dict(
                        o=lax.dot_general(p["qe"], p["st_old"].astype(BF16), _NT, preferred_element_type=F32),
                        att=lax.dot_general(p["q_d"], p["k_d"], _NT, preferred_element_type=F32),
                        cross=[lax.dot_general(q_t, k_t, _NT, preferred_element_type=F32)
                               for (q_t, k_t, _) in p["cross"]],
                        upd=lax.dot_general(p["vb16"], p["k_hat"], _TN, preferred_element_type=F32))
                if jobs:
                    jobs.pop(0)()
                for h in heads:
                    p, m = pre[h], mm[h]
                    att = jnp.where(same_sub, m["att"], 0.0)
                    for part, (_, _, s) in zip(m["cross"], p["cross"]):
                        att = att + (jnp.where((ri // (2 * s)) == (ci // (2 * s)), part, 0.0)
                                     if 2 * s < c_len else part)
                    m["att16"] = att.astype(BF16)
                for h in heads:
                    p, m = pre[h], mm[h]
                    os_[rs, p["hs"]] = m["o"] + _dot(m["att16"], p["vb16"])
                    stt[h] = p["st_old"] * jnp.exp(p["last"]) + m["upd"]
                if jobs:
                    jobs.pop(0)()
        while jobs:
            jobs.pop(0)()

    @pl.when(jnp.logical_not(fast))
    def _():
        for job in side_jobs:
            job()
        sub = HGRN_SUB
        row_in_sub = lax.broadcasted_iota(jnp.int32, (t_len, LANES), 0) % sub

        def att_general(qh, kh, cum):
            att = jnp.zeros((c_len, c_len), F32)
            for s in levels_down_to(sub):
                att = att + cross_level(qh, kh, cum, s)
            return att

        def head_body(h, carry):
            hs = pl.ds(pl.multiple_of(h * LANES, LANES), LANES)
            q_all = qs[:, hs]
            cum_all = cs[:, hs]
            o_diag = jnp.zeros((t_len, LANES), F32)
            for j in range(sub):
                def rows(ref):
                    return jnp.concatenate(
                        [jnp.broadcast_to(ref[sub * a + j:sub * a + j + 1, hs], (sub, LANES))
                         for a in range(t_len // sub)], axis=0)
                k_b, cum_b, v_b = rows(ks), rows(cs), rows(v_ref)
                arg = jnp.where(row_in_sub >= j, cum_all - cum_b, NEG)
                z = q_all * k_b * jnp.exp(arg)
                o_diag = o_diag + jnp.sum(z, axis=-1, keepdims=True) * v_b
            for c in range(n_chunks):
                chunk_head(h, hs, c, att_general, o_diag)
            return carry

        lax.fori_loop(0, nh, head_body, 0)

    o_ref[rows, :] = _gate_out_compute(os_, g_ref, nrm_ref, wout_ref, h_ref[rows, :], nh, False)


def _hgrn_prompt_kernel(h_ref, hn_ref, nmix_ref, win_ref, lb_ref, lbc_ref, nrm_ref, wout_ref, o_ref, st_ref,
                        us, qa, ka, ca, va, ga, wa, qb, kb, cb, vb, gb, wb, os_, stt, *, t_len, c_len):
    i = pl.program_id(1)
    rows_a, rows_b = slice(0, t_len), slice(t_len, 2 * t_len)
    buf_a = (qa, ka, ca, va, ga, wa)
    buf_b = (qb, kb, cb, vb, gb, wb)
    proj = (nmix_ref, win_ref, lb_ref, lbc_ref, us)

    @pl.when((i == 0) & (pl.program_id(0) == 0))
    def _():
        for job in _hgrn_project_jobs(h_ref[rows_a, :], *proj, buf_a, t_len, c_len):
            job()

    @pl.when(i == 0)
    def _():
        stt[...] = jnp.zeros_like(stt)

    jobs_b = _hgrn_project_jobs(h_ref[rows_b, :], *proj, buf_b, t_len, c_len)
    _hgrn_tile(h_ref, rows_a, buf_a, nrm_ref, wout_ref, o_ref, os_, stt, t_len, c_len, jobs_b)
    jobs_a = _hgrn_project_jobs(hn_ref[...], *proj, buf_a, t_len, c_len)
    _hgrn_tile(h_ref, rows_b, buf_b, nrm_ref, wout_ref, o_ref, os_, stt, t_len, c_len, jobs_a)

    @pl.when(i == pl.num_programs(1) - 1)
    def _():
        for h in range(HGRN_N_HEADS):
            st_ref[h] = stt[h].T


def _hgrn_prompt_layer(h, p, layer, j, batch, seq):
    t_len, c_len = HGRN_PROMPT_T, HGRN_PROMPT_C
    assert seq % (2 * t_len) == 0
    ns = seq // (2 * t_len)
    tok = lambda b, i: (b * ns + i, 0)
    nxt = lambda b, i: (2 * jnp.minimum(b * ns + i + 1, batch * ns - 1), 0)
    n_in = 2 * HGRN_F + 2 * D_MODEL
    wide = pltpu.VMEM((t_len, HGRN_F), F32)
    worst = pltpu.VMEM((1, LANES), F32)
    return pl.pallas_call(
        functools.partial(_hgrn_prompt_kernel, t_len=t_len, c_len=c_len),
        grid=(batch, ns),
        in_specs=[pl.BlockSpec((2 * t_len, D_MODEL), tok),
                  pl.BlockSpec((t_len, D_MODEL), nxt),
                  _layer_resident((1, D_MODEL), layer),
                  _layer_resident((D_MODEL, n_in), j),
                  _layer_resident((1, HGRN_F), j),
                  _layer_resident((1, HGRN_F), j),
                  _layer_resident((1, D_MODEL), j),
                  _layer_resident((D_MODEL, D_MODEL), j)],
        out_specs=[pl.BlockSpec((2 * t_len, D_MODEL), tok),
                   pl.BlockSpec((None, HGRN_N_HEADS, HGRN_DK, HGRN_DV), lambda b, i: (b, 0, 0, 0))],
        out_shape=[jax.ShapeDtypeStruct((batch * seq, D_MODEL), F32),
                   jax.ShapeDtypeStruct((batch, HGRN_N_HEADS, HGRN_DK, HGRN_DV), F32)],
        scratch_shapes=[pltpu.VMEM((t_len, D_MODEL), BF16)] + [wide] * 5 + [worst] + [wide] * 5 + [worst] + [
                        pltpu.VMEM((t_len, D_MODEL), F32),
                        pltpu.VMEM((HGRN_N_HEADS, HGRN_DV, HGRN_DK), F32)],
        compiler_params=_params(2),
        name="hgrn_prompt_layer",
    )(h, h, p["norm_mix"], p["hgrn_w_in"], p["hgrn_lb"], p["hgrn_lbc"], p["hgrn_norm"], p["hgrn_w_out"])


def _parked(i, l, n_blocks):
    return jnp.where(l == 0, i, n_blocks - 1)


def _state_specs(block, layer, n_layers, prev, n_blocks):
    nd = len(block) - 1
    zeros = (0,) * nd
    in_spec = pl.BlockSpec((None,) + block, lambda l, i: (layer, _parked(i, l, n_blocks)) + zeros)
    if prev is None:
        out_spec = pl.BlockSpec((None,) + block, lambda l, i: (l, i) + zeros)
        return in_spec, out_spec, n_layers
    out_spec = pl.BlockSpec((None,) + block, lambda l, i: (layer, i) + zeros)
    return in_spec, out_spec, 1


def _ssd_sample_body(xbc_ref, dt_ref, cst_ref, sst_ref, cw_ref, cb_ref, dtb_ref, a_ref, drep_ref,
                     e16_ref, e32_ref, y_ref, ncst_ref, nsst_ref, xc, xh, yi, n_tok, bb):
    nw = SSD_CONV_W
    ext = [cst_ref[s] for s in range(nw - 1)] + [xbc_ref[t] for t in range(n_tok)]
    for t in range(n_tok):
        acc = cb_ref[...] + ext[t] * cw_ref[0:1, :]
        for w in range(1, nw):
            acc = acc + ext[t + w] * cw_ref[w:w + 1, :]
        xc[t] = _silu(acc)
    for s in range(nw - 1):
        ncst_ref[s] = ext[n_tok + s]

    a_row = a_ref[...]
    dts, cums = [], []
    run = None
    for t in range(n_tok):
        dt = _softplus(dt_ref[t] + dtb_ref[...])
        run = dt * a_row if run is None else run + dt * a_row
        dts.append(dt)
        cums.append(run)

    lane = lax.broadcasted_iota(jnp.int32, (bb, LANES), 1)
    rpg = SSD_N_HEADS // SSD_N_GROUPS

    def group_dot(t, j):
        out = None
        for g in reversed(range(SSD_N_GROUPS)):
            cg = xc[t, :, SSD_D_INNER + SSD_GN + LANES * g:SSD_D_INNER + SSD_GN + LANES * (g + 1)]
            bg = xc[j, :, SSD_D_INNER + LANES * g:SSD_D_INNER + LANES * (g + 1)]
            sg = jnp.sum(cg * bg, axis=-1, keepdims=True)
            out = jnp.broadcast_to(sg, (bb, LANES)) if out is None else jnp.where(lane < rpg * (g + 1), sg, out)
        return out

    pairs = [(t, j) for t in range(n_tok) for j in range(t + 1)]
    w_rows = [group_dot(t, j) * jnp.exp(cums[t] - cums[j]) * dts[j] for (t, j) in pairs]
    tail_rows = [jnp.exp(cums[n_tok - 1] - cums[j]) * dts[j] for j in range(n_tok)]
    we = _dot(jnp.concatenate(w_rows + tail_rows, axis=0).astype(BF16), e16_ref[...])
    ce = _dot_exact(jnp.concatenate([jnp.exp(cums[t]) for t in range(n_tok)], axis=0), e32_ref[...])

    for j in range(n_tok):
        r0 = (len(pairs) + j) * bb
        xh[j] = we[r0:r0 + bb] * xc[j, :, 0:SSD_D_INNER]

    dec = jnp.exp(cums[n_tok - 1])
    gw = SSD_D_INNER // SSD_N_GROUPS
    for b in range(bb):
        for g in range(SSD_N_GROUPS):
            cb_ = xc[:, b, SSD_D_INNER + SSD_GN + LANES * g:SSD_D_INNER + SSD_GN + LANES * (g + 1)]
            bb_ = xc[:, b, SSD_D_INNER + LANES * g:SSD_D_INNER + LANES * (g + 1)]
            h0 = sst_ref[b, gw * g:gw * (g + 1), :]
            yi[:, b, gw * g:gw * (g + 1)] = lax.dot_general(
                cb_.astype(BF16), h0.astype(BF16), _NT, preferred_element_type=F32)
            xhb = xh[:, b, gw * g:gw * (g + 1)]
            upd = lax.dot_general(xhb.astype(BF16), bb_.astype(BF16), _TN, preferred_element_type=F32)
            for r in range(rpg):
                hh = rpg * g + r
                d = dec[b:b + 1, hh:hh + 1]
                rows = slice(SSD_HEAD_DIM * r, SSD_HEAD_DIM * (r + 1))
                nsst_ref[b, gw * g + SSD_HEAD_DIM * r:gw * g + SSD_HEAD_DIM * (r + 1), :] = h0[rows] * d + upd[rows]

    for t in range(n_tok):
        xt = xc[t, :, 0:SSD_D_INNER]
        y = xt * drep_ref[...] + yi[t] * ce[t * bb:(t + 1) * bb]
        for j in range(t + 1):
            r0 = pairs.index((t, j)) * bb
            y = y + we[r0:r0 + bb] * xc[j, :, 0:SSD_D_INNER]
        y_ref[t] = y


def _ssd_sample_kernel(*refs, n_tok, bb, has_prev):
    if has_prev:
        refs = refs[:11] + refs[12:]
    nsst_ref = refs[13]
    fill = pl.program_id(0)

    @pl.when(fill == 0)
    def _():
        _ssd_sample_body(*refs, n_tok, bb)

    if not has_prev:
        @pl.when(fill > 0)
        def _():
            nsst_ref[...] = jnp.zeros_like(nsst_ref)


def _ssd_sample_scan(xbc, dt_raw, cst, sst_all, prev, small, e16, e32, layer, n_tok, batch):
    bb = SAMPLE_BB
    rows = SSD_N_HEADS * SSD_HEAD_DIM
    n_layers = sst_all.shape[0]
    n_blocks = batch // bb
    tb = lambda l, i: (0, _parked(i, l, n_blocks), 0)
    st_in, st_out, n_fill = _state_specs((bb, rows, SSD_D_STATE), layer, n_layers, prev, n_blocks)
    in_specs = [pl.BlockSpec((n_tok, bb, SSD_CONV_DIM), tb),
                pl.BlockSpec((n_tok, bb, SSD_DT_PAD), tb),
                pl.BlockSpec((SSD_CONV_W - 1, bb, SSD_CONV_DIM), tb),
                st_in] + _ssd_small_specs(layer) + [
                _resident((LANES, SSD_D_INNER)),
                _resident((LANES, SSD_D_INNER))]
    args = [xbc, dt_raw, cst, sst_all, *small, e16, e32]
    aliases = {}
    if prev is not None:
        in_specs.append(pl.BlockSpec(memory_space=pl.ANY))
        args.append(prev)
        aliases = {len(args) - 1: 2}
    return pl.pallas_call(
        functools.partial(_ssd_sample_kernel, n_tok=n_tok, bb=bb, has_prev=prev is not None),
        grid=(n_fill, n_blocks),
        in_specs=in_specs,
        out_specs=[pl.BlockSpec((n_tok, bb, SSD_D_INNER), tb),
                   pl.BlockSpec((SSD_CONV_W - 1, bb, SSD_CONV_DIM), tb),
                   st_out],
        out_shape=[jax.ShapeDtypeStruct((n_tok, batch, SSD_D_INNER), F32),
                   jax.ShapeDtypeStruct((SSD_CONV_W - 1, batch, SSD_CONV_DIM), F32),
                   jax.ShapeDtypeStruct((n_layers, batch, rows, SSD_D_STATE), F32)],
        scratch_shapes=[pltpu.VMEM((n_tok, bb, SSD_CONV_DIM), F32),
                        pltpu.VMEM((n_tok, bb, SSD_D_INNER), F32),
                        pltpu.VMEM((n_tok, bb, SSD_D_INNER), F32)],
        input_output_aliases=aliases,
        compiler_params=_params(2),
        name="ssd_sample_scan",
    )(*args)


def _hgrn_sample_body(q_ref, f_ref, v_ref, st_ref, lb_ref, lbc_ref, o_ref, nst_ref,
                      qe, kh, oi, n_tok, bb):
    nh = HGRN_N_HEADS
    qs, ks, cums = [], [], []
    run = None
    for t in range(n_tok):
        q, k, logf = _hgrn_gates(q_ref[t], f_ref[t], lb_ref[...], lbc_ref[...])
        run = logf if run is None else run + logf
        qs.append(q)
        ks.append(k)
        cums.append(run)
    last = cums[n_tok - 1]
    for t in range(n_tok):
        qe[t] = qs[t] * jnp.exp(cums[t])
        kh[t] = ks[t] * jnp.exp(last - cums[t])
    dec = jnp.exp(last)

    for h in range(nh):
        hs = slice(LANES * h, LANES * (h + 1))
        dec_t = dec[:, hs].T
        for b in range(bb):
            s0 = st_ref[b, h]
            oi[:, b, hs] = _dot(qe[:, b, hs].astype(BF16), s0.astype(BF16))
            upd = lax.dot_general(kh[:, b, hs].astype(BF16), v_ref[:, b, hs].astype(BF16), _TN,
                                  preferred_element_type=F32)
            nst_ref[b, h] = s0 * dec_t[:, b:b + 1] + upd

    for t in range(n_tok):
        o = oi[t]
        for j in range(t + 1):
            z = qs[t] * ks[j] * jnp.exp(cums[t] - cums[j])
            vj = v_ref[j]
            parts = []
            for h in range(nh):
                hs = slice(LANES * h, LANES * (h + 1))
                parts.append(jnp.sum(z[:, hs], axis=-1, keepdims=True) * vj[:, hs])
            o = o + jnp.concatenate(parts, axis=-1)
        o_ref[t] = o


def _hgrn_sample_kernel(*refs, n_tok, bb, has_prev):
    if has_prev:
        refs = refs[:6] + refs[7:]
    nst_ref = refs[7]
    fill = pl.program_id(0)

    @pl.when(fill == 0)
    def _():
        _hgrn_sample_body(*refs, n_tok, bb)

    if not has_prev:
        @pl.when(fill > 0)
        def _():
            nst_ref[...] = jnp.zeros_like(nst_ref)


def _hgrn_sample_scan(qr, fz, v, st_all, prev, lb_all, lbc_all, layer, n_tok, batch):
    bb = SAMPLE_BB
    n_layers = st_all.shape[0]
    n_blocks = batch // bb
    tb = lambda l, i: (0, _parked(i, l, n_blocks), 0)
    st_in, st_out, n_fill = _state_specs((bb, HGRN_N_HEADS, HGRN_DK, HGRN_DV), layer, n_layers, prev, n_blocks)
    in_specs = [pl.BlockSpec((n_tok, bb, HGRN_F), tb),
                pl.BlockSpec((n_tok, bb, HGRN_F), tb),
                pl.BlockSpec((n_tok, bb, D_MODEL), tb),
                st_in,
                _layer_resident((1, HGRN_F), layer),
                _layer_resident((1, HGRN_F), layer)]
    args = [qr, fz, v, st_all, lb_all, lbc_all]
    aliases = {}
    if prev is not None:
        in_specs.append(pl.BlockSpec(memory_space=pl.ANY))
        args.append(prev)
        aliases = {len(args) - 1: 1}
    return pl.pallas_call(
        functools.partial(_hgrn_sample_kernel, n_tok=n_tok, bb=bb, has_prev=prev is not None),
        grid=(n_fill, n_blocks),
        in_specs=in_specs,
        out_specs=[pl.BlockSpec((n_tok, bb, D_MODEL), tb), st_out],
        out_shape=[jax.ShapeDtypeStruct((n_tok, batch, D_MODEL), F32),
                   jax.ShapeDtypeStruct((n_layers, batch, HGRN_N_HEADS, HGRN_DK, HGRN_DV), F32)],
        scratch_shapes=[pltpu.VMEM((n_tok, bb, HGRN_F), F32),
                        pltpu.VMEM((n_tok, bb, HGRN_F), F32),
                        pltpu.VMEM((n_tok, bb, D_MODEL), F32)],
        input_output_aliases=aliases,
        compiler_params=_params(2),
        name="hgrn_sample_scan",
    )(*args)


def _prep_params(norm_mix_w, norm_mlp_w, norm_f_w, ssd_w_in, ssd_conv_w, ssd_conv_b, ssd_dt_bias,
                 ssd_a_log, ssd_d, ssd_norm_w, ssd_w_out, hgrn_w_in, hgrn_lb_raw, hgrn_norm_w,
                 hgrn_w_out, mlp_w_up, mlp_w_down):
    la = ssd_w_in.shape[0]
    pad_dt = SSD_DT_PAD - SSD_N_HEADS
    p = {}
    p["norm_mix"] = norm_mix_w.reshape(DEPTH, 1, D_MODEL)
    p["norm_mlp"] = norm_mlp_w.reshape(DEPTH, 1, D_MODEL)
    p["norm_f"] = norm_f_w.reshape(1, D_MODEL)
    p["ssd_w_in"] = jnp.pad(ssd_w_in, ((0, 0), (0, 0), (0, pad_dt))).astype(BF16)
    p["ssd_small"] = (
        ssd_conv_w,
        ssd_conv_b.reshape(la, 1, SSD_CONV_DIM),
        jnp.pad(ssd_dt_bias, ((0, 0), (0, pad_dt))).reshape(la, 1, SSD_DT_PAD),
        jnp.pad(-jnp.exp(ssd_a_log.astype(F32)), ((0, 0), (0, pad_dt))).reshape(la, 1, SSD_DT_PAD),
        jnp.repeat(ssd_d.astype(F32), SSD_HEAD_DIM, axis=1).reshape(la, 1, SSD_D_INNER))
    p["ssd_norm"] = ssd_norm_w.reshape(la, 1, SSD_D_INNER)
    p["ssd_w_out"] = ssd_w_out.astype(BF16)
    p["hgrn_w_in"] = hgrn_w_in.astype(BF16)
    sm = jax.nn.softmax(hgrn_lb_raw.astype(F32), axis=0)
    lb = jnp.cumsum(sm, axis=0) - sm[0]
    lbn = lb.shape[0]
    p["hgrn_lb"] = lb.reshape(lbn, 1, HGRN_F)
    p["hgrn_lbc"] = jnp.maximum(lb, LB_FLOOR).reshape(lbn, 1, HGRN_F)
    p["hgrn_norm"] = hgrn_norm_w.reshape(lbn, 1, D_MODEL)
    p["hgrn_w_out"] = hgrn_w_out.astype(BF16)
    p["mlp_w_up"] = mlp_w_up.astype(BF16)
    p["mlp_w_down"] = mlp_w_down.astype(BF16)
    head_of_lane = jnp.arange(SSD_D_INNER, dtype=jnp.int32) // SSD_HEAD_DIM
    expand = (jnp.arange(LANES, dtype=jnp.int32)[:, None] == head_of_lane[None, :])
    p["expand16"] = expand.astype(BF16)
    p["expand32"] = expand.astype(F32)
    return p


def _trunk(h, p, prompt, batch, seq, conv_states, ssm_states, hgrn_states):
    tm = min(TOKEN_TILE, h.shape[0])
    new_conv, new_ssm, new_hgrn = [], [], []
    ssm_all, hgrn_all = None, None
    for layer in range(DEPTH):
        j = layer // 2
        if layer % 2 == 0 and prompt:
            h, cst, sst = _ssd_prompt_layer(h, p, layer, j, batch, seq)
            new_conv.append(cst)
            new_ssm.append(sst.reshape(batch, SSD_N_HEADS, SSD_HEAD_DIM, SSD_D_STATE))
        elif layer % 2 == 0:
            z, xbc, dt_raw = _norm_matmul(h, p["norm_mix"], layer, p["ssd_w_in"], j,
                                          (SSD_D_INNER, SSD_CONV_DIM, SSD_DT_PAD), tm, "ssd_in_proj")
            n_l = ssm_states.shape[0]
            y, cst, ssm_all = _ssd_sample_scan(
                xbc.reshape(seq, batch, SSD_CONV_DIM), dt_raw.reshape(seq, batch, SSD_DT_PAD),
                jnp.swapaxes(conv_states[j], 0, 1),
                ssm_states.reshape(n_l, batch, SSD_N_HEADS * SSD_HEAD_DIM, SSD_D_STATE), ssm_all,
                p["ssd_small"], p["expand16"], p["expand32"], j, seq, batch)
            new_conv.append(jnp.swapaxes(cst, 0, 1))
            h = _gate_out(y.reshape(seq * batch, SSD_D_INNER), z, p["ssd_norm"], p["ssd_w_out"], j, h,
                          SSD_N_GROUPS, True, tm, "ssd_out_proj")
        elif prompt:
            h, st = _hgrn_prompt_layer(h, p, layer, j, batch, seq)
            new_hgrn.append(st)
        else:
            qr, fz, v, g = _norm_matmul(h, p["norm_mix"], layer, p["hgrn_w_in"], j,
                                        (HGRN_F, HGRN_F, D_MODEL, D_MODEL), tm, "hgrn_in_proj")
            o, hgrn_all = _hgrn_sample_scan(
                qr.reshape(seq, batch, HGRN_F), fz.reshape(seq, batch, HGRN_F),
                v.reshape(seq, batch, D_MODEL), hgrn_states, hgrn_all,
                p["hgrn_lb"], p["hgrn_lbc"], j, seq, batch)
            h = _gate_out(o.reshape(seq * batch, D_MODEL), g, p["hgrn_norm"], p["hgrn_w_out"], j, h,
                          HGRN_N_HEADS, False, tm, "hgrn_out_proj")
        h = _mlp(h, p["norm_mlp"], p["mlp_w_up"], p["mlp_w_down"], layer, p["norm_f"],
                 layer == DEPTH - 1, tm, "mlp")
    if prompt:
        return h, jnp.stack(new_conv), jnp.stack(new_ssm), jnp.stack(new_hgrn)
    n_l = ssm_states.shape[0]
    return (h, jnp.stack(new_conv),
            ssm_all.reshape(n_l, batch, SSD_N_HEADS, SSD_HEAD_DIM, SSD_D_STATE), hgrn_all)


def kernel(x_prompt, x_sample, state_ssd_conv, state_ssd_ssm, state_hgrn, norm_mix_w, norm_mlp_w, norm_f_w,
           ssd_w_in, ssd_conv_w, ssd_conv_b, ssd_dt_bias, ssd_a_log, ssd_d, ssd_norm_w, ssd_w_out,
           hgrn_w_in, hgrn_lb_raw, hgrn_norm_w, hgrn_w_out, mlp_w_up, mlp_w_down):
    p = _prep_params(norm_mix_w, norm_mlp_w, norm_f_w, ssd_w_in, ssd_conv_w, ssd_conv_b, ssd_dt_bias,
                     ssd_a_log, ssd_d, ssd_norm_w, ssd_w_out, hgrn_w_in, hgrn_lb_raw, hgrn_norm_w,
                     hgrn_w_out, mlp_w_up, mlp_w_down)
    bp, lp, d = x_prompt.shape
    bs, ls, _ = x_sample.shape
    y_p, conv_p, ssm_p, hgrn_p = _trunk(x_prompt.reshape(bp * lp, d), p, True, bp, lp, None, None, None)
    xs = jnp.swapaxes(x_sample, 0, 1).reshape(ls * bs, d)
    y_s, conv_s, ssm_s, hgrn_s = _trunk(xs, p, False, bs, ls, state_ssd_conv, state_ssd_ssm, state_hgrn)
    y_s = jnp.swapaxes(y_s.reshape(ls, bs, d), 0, 1)
    return (y_p.reshape(bp, lp, d), y_s, conv_p, ssm_p, hgrn_p, conv_s, ssm_s, hgrn_s)
```

```python
import functools

import jax
import jax.numpy as jnp
from jax import lax
from jax.experimental import pallas as pl
from jax.experimental.pallas import tpu as pltpu

F32 = jnp.float32
BF16 = jnp.bfloat16

D_MODEL = 1024
DEPTH = 4
SSD_D_INNER = 2048
SSD_HEAD_DIM = 64
SSD_N_HEADS = 32
SSD_N_GROUPS = 4
SSD_D_STATE = 128
SSD_CONV_W = 4
SSD_GN = SSD_N_GROUPS * SSD_D_STATE
SSD_CONV_DIM = SSD_D_INNER + 2 * SSD_GN
SSD_DT_PAD = 128
HGRN_N_HEADS = 8
HGRN_DK = 128
HGRN_DV = 128
HGRN_F = HGRN_N_HEADS * HGRN_DK
D_FF = 4 * D_MODEL
EPS = 1e-5
LB_FLOOR = 1e-20
NEG = -1e30
LANES = 128
SUBLANES = 8
VMEM_LIMIT = 56 * 1024 * 1024

TOKEN_TILE = 512
SSD_PROMPT_T = 256
SSD_PROMPT_Q = 128
HGRN_PROMPT_T = 256
HGRN_PROMPT_C = 64
HGRN_SUB = 8
HGRN_FAST_SUB = 32
HGRN_FAST_RANGE = 75.0
SAMPLE_BB = 8

_NT = (((1,), (1,)), ((), ()))
_TN = (((0,), (0,)), ((), ()))


def _sigmoid(x):
    return jax.nn.sigmoid(x)


def _silu(x):
    return x * jax.nn.sigmoid(x)


def _softplus(x):
    return jnp.maximum(x, 0.0) + jnp.log1p(jnp.exp(-jnp.abs(x)))


def _rms(x, w):
    ms = jnp.mean(x * x, axis=-1, keepdims=True)
    return x * lax.rsqrt(ms + EPS) * w


def _dot(a, b):
    return jnp.dot(a, b, preferred_element_type=F32)


def _dot_exact(a, b):
    return jnp.dot(a, b, precision=lax.Precision.HIGHEST, preferred_element_type=F32)


def _cumsum_rows(tril, x):
    hi = x.astype(BF16)
    r1 = x - hi.astype(F32)
    mid = r1.astype(BF16)
    lo = (r1 - mid.astype(F32)).astype(BF16)
    return _dot(tril, hi) + _dot(tril, mid) + _dot(tril, lo)


def _params(n_axes):
    return pltpu.CompilerParams(dimension_semantics=("arbitrary",) * n_axes,
                                vmem_limit_bytes=VMEM_LIMIT)


def _resident(shape):
    nd = len(shape)
    return pl.BlockSpec(shape, lambda *_: (0,) * nd, pipeline_mode=pl.Buffered(1))


def _layer_resident(shape, layer):
    nd = len(shape)
    return pl.BlockSpec((None,) + tuple(shape), lambda *_: (layer,) + (0,) * nd,
                        pipeline_mode=pl.Buffered(1))


def _norm_matmul_kernel(x_ref, nw_ref, w_ref, *o_refs):
    u = _rms(x_ref[...], nw_ref[...]).astype(BF16)
    c = 0
    for o_ref in o_refs:
        n = o_ref.shape[-1]
        o_ref[...] = _dot(u, w_ref[:, c:c + n])
        c += n


def _norm_matmul(x, nw_all, nw_layer, w_all, w_layer, splits, tm, name):
    m, d = x.shape
    n_tot = w_all.shape[-1]
    assert sum(splits) == n_tot and m % tm == 0
    return pl.pallas_call(
        _norm_matmul_kernel,
        grid=(m // tm,),
        in_specs=[pl.BlockSpec((tm, d), lambda i: (i, 0)),
                  _layer_resident((1, d), nw_layer),
                  _layer_resident((d, n_tot), w_layer)],
        out_specs=[pl.BlockSpec((tm, n), lambda i: (i, 0)) for n in splits],
        out_shape=[jax.ShapeDtypeStruct((m, n), F32) for n in splits],
        compiler_params=_params(1),
        name=name,
    )(x, nw_all, w_all)


def _gate_out_groups(y_ref, g_ref, nw_ref, w_ref, k0, k1, n_groups, gate_first):
    gs = y_ref.shape[-1] // n_groups
    parts = []
    for k in range(k0, k1):
        sl = slice(k * gs, (k + 1) * gs)
        yk = y_ref[:, sl]
        sg = _silu(g_ref[:, sl])
        if gate_first:
            yk = yk * sg
        ms = jnp.mean(yk * yk, axis=-1, keepdims=True)
        yn = yk * lax.rsqrt(ms + EPS) * nw_ref[:, sl]
        if not gate_first:
            yn = yn * sg
        parts.append(yn.astype(BF16))
    lhs = parts[0] if len(parts) == 1 else jnp.concatenate(parts, axis=1)
    return _dot(lhs, w_ref[k0 * gs:k1 * gs, :])


def _gate_out_compute(y_ref, g_ref, nw_ref, w_ref, acc, n_groups, gate_first):
    for k in range(n_groups):
        acc = acc + _gate_out_groups(y_ref, g_ref, nw_ref, w_ref, k, k + 1, n_groups, gate_first)
    return acc


def _gate_out_kernel(y_ref, g_ref, nw_ref, w_ref, h_ref, o_ref, *, n_groups, gate_first):
    o_ref[...] = _gate_out_compute(y_ref, g_ref, nw_ref, w_ref, h_ref[...], n_groups, gate_first)


def _gate_out(y, g, nw_all, w_all, layer, h, n_groups, gate_first, tm, name):
    m, di = y.shape
    d = h.shape[1]
    return pl.pallas_call(
        functools.partial(_gate_out_kernel, n_groups=n_groups, gate_first=gate_first),
        grid=(m // tm,),
        in_specs=[pl.BlockSpec((tm, di), lambda i: (i, 0)),
                  pl.BlockSpec((tm, di), lambda i: (i, 0)),
                  _layer_resident((1, di), layer),
                  _layer_resident((di, d), layer),
                  pl.BlockSpec((tm, d), lambda i: (i, 0))],
        out_specs=pl.BlockSpec((tm, d), lambda i: (i, 0)),
        out_shape=jax.ShapeDtypeStruct((m, d), F32),
        compiler_params=_params(1),
        name=name,
    )(y, g, nw_all, w_all, h)


def _mlp_kernel(x_ref, nw_ref, wu_ref, wd_ref, nf_ref, o_ref, *, final_norm, f_chunk):
    x = x_ref[...]
    u = _rms(x, nw_ref[...]).astype(BF16)
    acc = x
    for f in range(0, D_FF, f_chunk):
        mid = _dot(u, wu_ref[:, f:f + f_chunk])
        mid = jnp.square(jnp.maximum(mid, 0.0)).astype(BF16)
        acc = acc + _dot(mid, wd_ref[f:f + f_chunk, :])
    if final_norm:
        acc = _rms(acc, nf_ref[...])
    o_ref[...] = acc


def _mlp(x, nw_all, wu_all, wd_all, layer, nf, final_norm, tm, name):
    m, d = x.shape
    return pl.pallas_call(
        functools.partial(_mlp_kernel, final_norm=final_norm, f_chunk=1024),
        grid=(m // tm,),
        in_specs=[pl.BlockSpec((tm, d), lambda i: (i, 0)),
                  _layer_resident((1, d), layer),
                  _layer_resident((d, D_FF), layer),
                  _layer_resident((D_FF, d), layer),
                  _resident((1, d))],
        out_specs=pl.BlockSpec((tm, d), lambda i: (i, 0)),
        out_shape=jax.ShapeDtypeStruct((m, d), F32),
        compiler_params=_params(1),
        name=name,
    )(x, nw_all, wu_all, wd_all, nf)


def _ssd_small_specs(layer):
    return [_layer_resident((SSD_CONV_W, SSD_CONV_DIM), layer),
            _layer_resident((1, SSD_CONV_DIM), layer),
            _layer_resident((1, SSD_DT_PAD), layer),
            _layer_resident((1, SSD_DT_PAD), layer),
            _layer_resident((1, SSD_D_INNER), layer)]


def _ssd_project_jobs(h_tile, nmix_ref, win_ref, us, xext, zs, dts, t_len, n_blk=256):
    cdim = SSD_CONV_DIM
    us[...] = _rms(h_tile, nmix_ref[...]).astype(BF16)

    def job(dst, rows, c_dst, c_src, n):
        def run():
            dst[rows, c_dst:c_dst + n] = _dot(us[...], win_ref[:, c_src:c_src + n])
        return run

    every = slice(None)
    jobs = [job(dts, every, 0, SSD_D_INNER + cdim, SSD_DT_PAD)]
    jobs += [job(xext, slice(SUBLANES, SUBLANES + t_len), c, SSD_D_INNER + c, n_blk) for c in range(0, cdim, n_blk)]
    jobs += [job(zs, every, c, c, n_blk) for c in range(0, SSD_D_INNER, n_blk)]
    return jobs


def _ssd_tile(h_ref, xext, xext_next, zs, dts, cw_ref, cb_ref, dtb_ref, a_ref, drep_ref, nrm_ref, wout_ref,
              o_ref, rows_out, cst_ref, xc, ys, cq, ht, t_len, q_len, side_jobs):
    jobs = list(side_jobs)
    cdim = SSD_CONV_DIM
    off = SUBLANES - (SSD_CONV_W - 1)
    for c0 in range(0, cdim, 512):
        cs = slice(c0, c0 + 512)
        acc = cb_ref[:, cs] + xext[off:off + t_len, cs] * cw_ref[0:1, cs]
        for w in range(1, SSD_CONV_W):
            acc = acc + xext[off + w:off + w + t_len, cs] * cw_ref[w:w + 1, cs]
        xc[:, cs] = _silu(acc)
    cst_ref[...] = xext[t_len + off:t_len + SUBLANES, :]
    xext_next[0:SUBLANES, :] = xext[t_len:t_len + SUBLANES, :]

    dt_all = _softplus(dts[...] + dtb_ref[...])
    a_row = a_ref[...]
    ri = lax.broadcasted_iota(jnp.int32, (q_len, q_len), 0)
    ci = lax.broadcasted_iota(jnp.int32, (q_len, q_len), 1)
    causal = ri >= ci
    tril = jnp.where(causal, 1.0, 0.0).astype(BF16)
    lane = lax.broadcasted_iota(jnp.int32, (1, LANES), 1)
    lo_mask = lane < SSD_HEAD_DIM

    n_chunks = t_len // q_len
    for c in range(n_chunks):
        dtc = dt_all[c * q_len:(c + 1) * q_len]
        cum_c = _cumsum_rows(tril, dtc * a_row)
        cq[c, 0] = cum_c
        cq[c, 1] = cum_c.T
        cq[c, 2] = dtc.T

    o_ref[rows_out, :] = h_ref[rows_out, :]

    def out_proj_job(g):
        def run():
            o_ref[rows_out, :] += _gate_out_groups(ys, zs, nrm_ref, wout_ref, g, g + 1, SSD_N_GROUPS, True)
        return run

    for g in range(SSD_N_GROUPS):
        for c in range(n_chunks):
            rs = slice(c * q_len, (c + 1) * q_len)
            cum, cum_t, dt_t = cq.at[c, 0], cq.at[c, 1], cq.at[c, 2]
            b_g = xc[rs, SSD_D_INNER + LANES * g:SSD_D_INNER + LANES * (g + 1)]
            c_g = xc[rs, SSD_D_INNER + SSD_GN + LANES * g:SSD_D_INNER + SSD_GN + LANES * (g + 1)]
            cbm = lax.dot_general(c_g.astype(BF16), b_g.astype(BF16), _NT, preferred_element_type=F32)
            b_gt = b_g.T
            for m in range(SSD_N_HEADS // SSD_N_GROUPS // 2):
                p = g * 4 + m
                ps = slice(LANES * p, LANES * (p + 1))
                xp = xc[rs, ps]
                h_old = ht[p]
                w_mats, c_exps, b_ss, x_ss, h_ss, decs = [], [], [], [], [], []
                for s in range(2):
                    hh = 2 * p + s
                    msk = lo_mask if s == 0 else jnp.logical_not(lo_mask)
                    col = cum[:, hh:hh + 1]
                    row = cum_t[hh:hh + 1, :]
                    dt_row = dt_t[hh:hh + 1, :]
                    dec = jnp.exp(jnp.where(causal, col - row, NEG))
                    w_mats.append((cbm * dec * dt_row).astype(BF16))
                    c_exps.append((c_g * jnp.exp(col)).astype(BF16))
                    x_ss.append(jnp.where(msk, xp, 0.0).astype(BF16))
                    h_ss.append(jnp.where(msk, h_old, 0.0).astype(BF16))
                    last = cum[q_len - 1:q_len, hh:hh + 1]
                    b_ss.append((b_gt * (jnp.exp(last - row) * dt_row)).astype(BF16))
                    decs.append(jnp.exp(last))
                x_cat = jnp.concatenate(x_ss, axis=0)
                y_acc = (xp * drep_ref[:, ps]
                         + _dot(jnp.concatenate(w_mats + c_exps, axis=1),
                                jnp.concatenate([x_cat] + h_ss, axis=0)))
                dec_row = jnp.where(lo_mask, decs[0], decs[1])
                ht[p] = h_old * dec_row + _dot(jnp.concatenate(b_ss, axis=1), x_cat)
                ys[rs, ps] = y_acc
                if jobs:
                    jobs.pop(0)()
        jobs.insert(0, out_proj_job(g))

    while jobs:
        jobs.pop(0)()


def _ssd_prompt_kernel(h_ref, hn_ref, nmix_ref, win_ref, cw_ref, cb_ref, dtb_ref, a_ref, drep_ref, nrm_ref, wout_ref,
                       o_ref, cst_ref, sst_ref, us, xa, xb, za, zb, da, db, xc, ys, cq, ht, *, t_len, q_len):
    i = pl.program_id(1)
    rows_a, rows_b = slice(0, t_len), slice(t_len, 2 * t_len)
    rest = (cw_ref, cb_ref, dtb_ref, a_ref, drep_ref, nrm_ref, wout_ref)

    @pl.when((i == 0) & (pl.program_id(0) == 0))
    def _():
        for job in _ssd_project_jobs(h_ref[rows_a, :], nmix_ref, win_ref, us, xa, za, da, t_len):
            job()

    @pl.when(i == 0)
    def _():
        xa[0:SUBLANES, :] = jnp.zeros((SUBLANES, SSD_CONV_DIM), F32)
        ht[...] = jnp.zeros_like(ht)

    jobs_b = _ssd_project_jobs(h_ref[rows_b, :], nmix_ref, win_ref, us, xb, zb, db, t_len)
    _ssd_tile(h_ref, xa, xb, za, da, *rest, o_ref, rows_a, cst_ref, xc, ys, cq, ht, t_len, q_len, jobs_b)
    jobs_a = _ssd_project_jobs(hn_ref[...], nmix_ref, win_ref, us, xa, za, da, t_len)
    _ssd_tile(h_ref, xb, xa, zb, db, *rest, o_ref, rows_b, cst_ref, xc, ys, cq, ht, t_len, q_len, jobs_a)

    @pl.when(i == pl.num_programs(1) - 1)
    def _():
        for p in range(SSD_N_HEADS // 2):
            sst_ref[LANES * p:LANES * (p + 1), :] = ht[p].T


def _ssd_prompt_layer(h, p, layer, j, batch, seq):
    t_len, q_len = SSD_PROMPT_T, SSD_PROMPT_Q
    assert seq % (2 * t_len) == 0
    ns = seq // (2 * t_len)
    tok = lambda b, i: (b * ns + i, 0)
    nxt = lambda b, i: (2 * jnp.minimum(b * ns + i + 1, batch * ns - 1), 0)
    n_in = SSD_D_INNER + SSD_CONV_DIM + SSD_DT_PAD
    xbuf = pltpu.VMEM((t_len + SUBLANES, SSD_CONV_DIM), F32)
    zbuf = pltpu.VMEM((t_len, SSD_D_INNER), F32)
    dbuf = pltpu.VMEM((t_len, SSD_DT_PAD), F32)
    return pl.pallas_call(
        functools.partial(_ssd_prompt_kernel, t_len=t_len, q_len=q_len),
        grid=(batch, ns),
        in_specs=[pl.BlockSpec((2 * t_len, D_MODEL), tok),
                  pl.BlockSpec((t_len, D_MODEL), nxt),
                  _layer_resident((1, D_MODEL), layer),
                  _layer_resident((D_MODEL, n_in), j)] + _ssd_small_specs(j) + [
                  _layer_resident((1, SSD_D_INNER), j),
                  _layer_resident((SSD_D_INNER, D_MODEL), j)],
        out_specs=[pl.BlockSpec((2 * t_len, D_MODEL), tok),
                   pl.BlockSpec((None, SSD_CONV_W - 1, SSD_CONV_DIM), lambda b, i: (b, 0, 0)),
                   pl.BlockSpec((None, SSD_N_HEADS * SSD_HEAD_DIM, SSD_D_STATE), lambda b, i: (b, 0, 0))],
        out_shape=[jax.ShapeDtypeStruct((batch * seq, D_MODEL), F32),
                   jax.ShapeDtypeStruct((batch, SSD_CONV_W - 1, SSD_CONV_DIM), F32),
                   jax.ShapeDtypeStruct((batch, SSD_N_HEADS * SSD_HEAD_DIM, SSD_D_STATE), F32)],
        scratch_shapes=[pltpu.VMEM((t_len, D_MODEL), BF16), xbuf, xbuf, zbuf, zbuf, dbuf, dbuf,
                        pltpu.VMEM((t_len, SSD_CONV_DIM), F32),
                        pltpu.VMEM((t_len, SSD_D_INNER), F32),
                        pltpu.VMEM((t_len // q_len, 3, q_len, LANES), F32),
                        pltpu.VMEM((SSD_N_HEADS // 2, SSD_D_STATE, LANES), F32)],
        compiler_params=_params(2),
        name="ssd_prompt_layer",
    )(h, h, p["norm_mix"], p["ssd_w_in"], *p["ssd_small"], p["ssd_norm"], p["ssd_w_out"])


def _hgrn_gates(q_raw, fz, lb, lbc):
    e = jnp.exp(-jnp.abs(fz))
    r = 1.0 / (1.0 + e)
    er = e * r
    pos = fz >= 0.0
    sig_pos = jnp.where(pos, r, er)
    sig_neg = jnp.where(pos, er, r)
    logf = jnp.log(sig_pos + lbc * sig_neg)
    return _silu(q_raw), (1.0 - lb) * sig_neg, logf


def _hgrn_project_jobs(h_tile, nmix_ref, win_ref, lb_ref, lbc_ref, us, buf, t_len, c_len, n_blk=256):
    qs, ks, cs, vs, gs, ws = buf
    fsub = HGRN_FAST_SUB
    us[...] = _rms(h_tile, nmix_ref[...]).astype(BF16)
    ri = lax.broadcasted_iota(jnp.int32, (c_len, c_len), 0)
    ci = lax.broadcasted_iota(jnp.int32, (c_len, c_len), 1)
    tril = jnp.where(ri >= ci, 1.0, 0.0).astype(BF16)

    def gate_job(c0):
        def run():
            cols = slice(c0, c0 + n_blk)
            fz = _dot(us[...], win_ref[:, HGRN_F + c0:HGRN_F + c0 + n_blk])
            qr = _dot(us[...], win_ref[:, c0:c0 + n_blk])
            q, k, logf = _hgrn_gates(qr, fz, lb_ref[:, cols], lbc_ref[:, cols])
            qs[:, cols] = q
            ks[:, cols] = k
            worst = None
            for c in range(t_len // c_len):
                cum = _cumsum_rows(tril, logf[c * c_len:(c + 1) * c_len])
                cs[c * c_len:(c + 1) * c_len, cols] = cum
                prev = None
                for a in range(c_len // fsub):
                    end = cum[(a + 1) * fsub - 1:(a + 1) * fsub, :]
                    drop = -end if prev is None else prev - end
                    worst = drop if worst is None else jnp.maximum(worst, drop)
                    prev = end
            w = worst[:, 0:LANES]
            for l0 in range(LANES, n_blk, LANES):
                w = jnp.maximum(w, worst[:, l0:l0 + LANES])
            ws[...] = w if c0 == 0 else jnp.maximum(ws[...], w)
        return run

    def plain_job(dst, c_src, c0):
        def run():
            dst[:, c0:c0 + n_blk] = _dot(us[...], win_ref[:, c_src + c0:c_src + c0 + n_blk])
        return run

    jobs = []
    for c0 in range(0, HGRN_F, n_blk):
        jobs += [gate_job(c0), plain_job(vs, 2 * HGRN_F, c0), plain_job(gs, 2 * HGRN_F + D_MODEL, c0)]
    return jobs


def _hgrn_tile(h_ref, rows, buf, nrm_ref, wout_ref, o_ref, os_, stt, t_len, c_len, side_jobs):
    qs, ks, cs, v_ref, g_ref, ws = buf
    nh = HGRN_N_HEADS
    n_chunks = t_len // c_len
    fsub = HGRN_FAST_SUB
    ri = lax.broadcasted_iota(jnp.int32, (c_len, c_len), 0)
    ci = lax.broadcasted_iota(jnp.int32, (c_len, c_len), 1)
    causal = ri >= ci
    fast = jnp.max(ws[...]) <= HGRN_FAST_RANGE

    o_ref[rows, :] = h_ref[rows, :]

    def out_proj_job(hp):
        def run():
            o_ref[rows, :] += _gate_out_groups(os_, g_ref, nrm_ref, wout_ref, 2 * hp, 2 * hp + 2, nh, False)
        return run

    row_c = lax.broadcasted_iota(jnp.int32, (c_len, LANES), 0)

    def chunk_head(h, hs, c, att_fn, o_extra):
        rs = slice(c * c_len, (c + 1) * c_len)
        qh, kh, vh, cum = qs[rs, hs], ks[rs, hs], v_ref[rs, hs], cs[rs, hs]
        st_old = stt[h]
        vb16 = vh.astype(BF16)
        o = lax.dot_general((qh * jnp.exp(cum)).astype(BF16), st_old.astype(BF16), _NT,
                            preferred_element_type=F32)
        o = o + _dot(att_fn(qh, kh, cum).astype(BF16), vb16)
        if o_extra is not None:
            o = o + o_extra[rs]
        os_[rs, hs] = o
        last = cum[c_len - 1:c_len, :]
        k_hat = (kh * jnp.exp(last - cum)).astype(BF16)
        upd = lax.dot_general(vb16, k_hat, _TN, preferred_element_type=F32)
        stt[h] = st_old * jnp.exp(last) + upd

    def cross_operands(qh, kh, cum, s):
        ref_rows = jnp.concatenate(
            [jnp.broadcast_to(cum[blk * 2 * s + s - 1:blk * 2 * s + s, :], (2 * s, LANES))
             for blk in range(c_len // (2 * s))], axis=0)
        second = (row_c // s) % 2 == 1
        e = jnp.exp(jnp.where(second, cum - ref_rows, ref_rows - cum))
        q_t = jnp.where(second, qh * e, 0.0).astype(BF16)
        k_t = jnp.where(second, 0.0, kh * e).astype(BF16)
        return q_t, k_t, s

    def cross_level(qh, kh, cum, s):
        q_t, k_t, _ = cross_operands(qh, kh, cum, s)
        part = lax.dot_general(q_t, k_t, _NT, preferred_element_type=F32)
        if 2 * s < c_len:
            part = jnp.where((ri // (2 * s)) == (ci // (2 * s)), part, 0.0)
        return part

    def levels_down_to(sub):
        out = []
        s = c_len // 2
        while s >= sub:
            out.append(s)
            s //= 2
        return out

    @pl.when(fast)
    def _():
        jobs = list(side_jobs)
        same_sub = causal & ((ri // fsub) == (ci // fsub))
        for h0 in range(0, nh, 4):
            heads = range(h0, h0 + 4)
            for c in range(n_chunks):
                rs = slice(c * c_len, (c + 1) * c_len)
                pre = {}
                for h in heads:
                    hs = slice(LANES * h, LANES * (h + 1))
                    qh, kh, cum = qs[rs, hs], ks[rs, hs], cs[rs, hs]
                    start = jnp.concatenate(
                        [jnp.zeros((fsub, LANES), F32)] +
                        [jnp.broadcast_to(cum[a * fsub - 1:a * fsub, :], (fsub, LANES))
                         for a in range(1, c_len // fsub)], axis=0)
                    last = cum[c_len - 1:c_len, :]
                    pre[h] = dict(
                        hs=hs, last=last, st_old=stt[h], vb16=v_ref[rs, hs].astype(BF16),
                        qe=(qh * jnp.exp(cum)).astype(BF16),
                        q_d=(qh * jnp.exp(cum - start)).astype(BF16),
                        k_d=(kh * jnp.exp(start - cum)).astype(BF16),
                        k_hat=(kh * jnp.exp(last - cum)).astype(BF16),
                        cross=[cross_operands(qh, kh, cum, s) for s in levels_down_to(fsub)])
                if jobs:
                    jobs.pop(0)()
                mm = {}
                for h in heads:
                    p = pre[h]
                    mm[h] = dict(
                        o=lax.dot_general(p["qe"], p["st_old"].astype(BF16), _NT, preferred_element_type=F32),
                        att=lax.dot_general(p["q_d"], p["k_d"], _NT, preferred_element_type=F32),
                        cross=[lax.dot_general(q_t, k_t, _NT, preferred_element_type=F32)
                               for (q_t, k_t, _) in p["cross"]],
                        upd=lax.dot_general(p["vb16"], p["k_hat"], _TN, preferred_element_type=F32))
                if jobs:
                    jobs.pop(0)()
                for h in heads:
                    p, m = pre[h], mm[h]
                    att = jnp.where(same_sub, m["att"], 0.0)
                    for part, (_, _, s) in zip(m["cross"], p["cross"]):
                        att = att + (jnp.where((ri // (2 * s)) == (ci // (2 * s)), part, 0.0)
                                     if 2 * s < c_len else part)
                    m["att16"] = att.astype(BF16)
                for h in heads:
                    p, m = pre[h], mm[h]
                    os_[rs, p["hs"]] = m["o"] + _dot(m["att16"], p["vb16"])
                    stt[h] = p["st_old"] * jnp.exp(p["last"]) + m["upd"]
                if jobs:
                    jobs.pop(0)()
            jobs[0:0] = [out_proj_job(hp) for hp in range(h0 // 2, h0 // 2 + 2)]
        while jobs:
            jobs.pop(0)()

    @pl.when(jnp.logical_not(fast))
    def _():
        for job in side_jobs:
            job()
        sub = HGRN_SUB
        row_in_sub = lax.broadcasted_iota(jnp.int32, (t_len, LANES), 0) % sub

        def att_general(qh, kh, cum):
            att = jnp.zeros((c_len, c_len), F32)
            for s in levels_down_to(sub):
                att = att + cross_level(qh, kh, cum, s)
            return att

        def head_body(h, carry):
            hs = pl.ds(pl.multiple_of(h * LANES, LANES), LANES)
            q_all = qs[:, hs]
            cum_all = cs[:, hs]
            o_diag = jnp.zeros((t_len, LANES), F32)
            for j in range(sub):
                def rows(ref):
                    return jnp.concatenate(
                        [jnp.broadcast_to(ref[sub * a + j:sub * a + j + 1, hs], (sub, LANES))
                         for a in range(t_len // sub)], axis=0)
                k_b, cum_b, v_b = rows(ks), rows(cs), rows(v_ref)
                arg = jnp.where(row_in_sub >= j, cum_all - cum_b, NEG)
                z = q_all * k_b * jnp.exp(arg)
                o_diag = o_diag + jnp.sum(z, axis=-1, keepdims=True) * v_b
            for c in range(n_chunks):
                chunk_head(h, hs, c, att_general, o_diag)
            return carry

        lax.fori_loop(0, nh, head_body, 0)
        for hp in range(nh // 2):
            out_proj_job(hp)()


def _hgrn_prompt_kernel(h_ref, hn_ref, nmix_ref, win_ref, lb_ref, lbc_ref, nrm_ref, wout_ref, o_ref, st_ref,
                        us, qa, ka, ca, va, ga, wa, qb, kb, cb, vb, gb, wb, os_, stt, *, t_len, c_len):
    i = pl.program_id(1)
    rows_a, rows_b = slice(0, t_len), slice(t_len, 2 * t_len)
    buf_a = (qa, ka, ca, va, ga, wa)
    buf_b = (qb, kb, cb, vb, gb, wb)
    proj = (nmix_ref, win_ref, lb_ref, lbc_ref, us)

    @pl.when((i == 0) & (pl.program_id(0) == 0))
    def _():
        for job in _hgrn_project_jobs(h_ref[rows_a, :], *proj, buf_a, t_len, c_len):
            job()

    @pl.when(i == 0)
    def _():
        stt[...] = jnp.zeros_like(stt)

    jobs_b = _hgrn_project_jobs(h_ref[rows_b, :], *proj, buf_b, t_len, c_len)
    _hgrn_tile(h_ref, rows_a, buf_a, nrm_ref, wout_ref, o_ref, os_, stt, t_len, c_len, jobs_b)
    jobs_a = _hgrn_project_jobs(hn_ref[...], *proj, buf_a, t_len, c_len)
    _hgrn_tile(h_ref, rows_b, buf_b, nrm_ref, wout_ref, o_ref, os_, stt, t_len, c_len, jobs_a)

    @pl.when(i == pl.num_programs(1) - 1)
    def _():
        for h in range(HGRN_N_HEADS):
            st_ref[h] = stt[h].T


def _hgrn_prompt_layer(h, p, layer, j, batch, seq):
    t_len, c_len = HGRN_PROMPT_T, HGRN_PROMPT_C
    assert seq % (2 * t_len) == 0
    ns = seq // (2 * t_len)
    tok = lambda b, i: (b * ns + i, 0)
    nxt = lambda b, i: (2 * jnp.minimum(b * ns + i + 1, batch * ns - 1), 0)
    n_in = 2 * HGRN_F + 2 * D_MODEL
    wide = pltpu.VMEM((t_len, HGRN_F), F32)
    worst = pltpu.VMEM((1, LANES), F32)
    return pl.pallas_call(
        functools.partial(_hgrn_prompt_kernel, t_len=t_len, c_len=c_len),
        grid=(batch, ns),
        in_specs=[pl.BlockSpec((2 * t_len, D_MODEL), tok),
                  pl.BlockSpec((t_len, D_MODEL), nxt),
                  _layer_resident((1, D_MODEL), layer),
                  _layer_resident((D_MODEL, n_in), j),
                  _layer_resident((1, HGRN_F), j),
                  _layer_resident((1, HGRN_F), j),
                  _layer_resident((1, D_MODEL), j),
                  _layer_resident((D_MODEL, D_MODEL), j)],
        out_specs=[pl.BlockSpec((2 * t_len, D_MODEL), tok),
                   pl.BlockSpec((None, HGRN_N_HEADS, HGRN_DK, HGRN_DV), lambda b, i: (b, 0, 0, 0))],
        out_shape=[jax.ShapeDtypeStruct((batch * seq, D_MODEL), F32),
                   jax.ShapeDtypeStruct((batch, HGRN_N_HEADS, HGRN_DK, HGRN_DV), F32)],
        scratch_shapes=[pltpu.VMEM((t_len, D_MODEL), BF16)] + [wide] * 5 + [worst] + [wide] * 5 + [worst] + [
                        pltpu.VMEM((t_len, D_MODEL), F32),
                        pltpu.VMEM((HGRN_N_HEADS, HGRN_DV, HGRN_DK), F32)],
        compiler_params=_params(2),
        name="hgrn_prompt_layer",
    )(h, h, p["norm_mix"], p["hgrn_w_in"], p["hgrn_lb"], p["hgrn_lbc"], p["hgrn_norm"], p["hgrn_w_out"])


def _parked(i, l, n_blocks):
    return jnp.where(l == 0, i, n_blocks - 1)


def _state_specs(block, layer, n_layers, prev, n_blocks):
    nd = len(block) - 1
    zeros = (0,) * nd
    in_spec = pl.BlockSpec((None,) + block, lambda l, i: (layer, _parked(i, l, n_blocks)) + zeros)
    if prev is None:
        out_spec = pl.BlockSpec((None,) + block, lambda l, i: (l, i) + zeros)
        return in_spec, out_spec, n_layers
    out_spec = pl.BlockSpec((None,) + block, lambda l, i: (layer, i) + zeros)
    return in_spec, out_spec, 1


def _ssd_sample_body(xbc_ref, dt_ref, cst_ref, sst_ref, cw_ref, cb_ref, dtb_ref, a_ref, drep_ref,
                     e16_ref, e32_ref, y_ref, ncst_ref, nsst_ref, xc, xh, yi, n_tok, bb):
    nw = SSD_CONV_W
    ext = [cst_ref[s] for s in range(nw - 1)] + [xbc_ref[t] for t in range(n_tok)]
    for t in range(n_tok):
        acc = cb_ref[...] + ext[t] * cw_ref[0:1, :]
        for w in range(1, nw):
            acc = acc + ext[t + w] * cw_ref[w:w + 1, :]
        xc[t] = _silu(acc)
    for s in range(nw - 1):
        ncst_ref[s] = ext[n_tok + s]

    a_row = a_ref[...]
    dts, cums = [], []
    run = None
    for t in range(n_tok):
        dt = _softplus(dt_ref[t] + dtb_ref[...])
        run = dt * a_row if run is None else run + dt * a_row
        dts.append(dt)
        cums.append(run)

    lane = lax.broadcasted_iota(jnp.int32, (bb, LANES), 1)
    rpg = SSD_N_HEADS // SSD_N_GROUPS

    def group_dot(t, j):
        out = None
        for g in reversed(range(SSD_N_GROUPS)):
            cg = xc[t, :, SSD_D_INNER + SSD_GN + LANES * g:SSD_D_INNER + SSD_GN + LANES * (g + 1)]
            bg = xc[j, :, SSD_D_INNER + LANES * g:SSD_D_INNER + LANES * (g + 1)]
            sg = jnp.sum(cg * bg, axis=-1, keepdims=True)
            out = jnp.broadcast_to(sg, (bb, LANES)) if out is None else jnp.where(lane < rpg * (g + 1), sg, out)
        return out

    pairs = [(t, j) for t in range(n_tok) for j in range(t + 1)]
    w_rows = [group_dot(t, j) * jnp.exp(cums[t] - cums[j]) * dts[j] for (t, j) in pairs]
    tail_rows = [jnp.exp(cums[n_tok - 1] - cums[j]) * dts[j] for j in range(n_tok)]
    we = _dot(jnp.concatenate(w_rows + tail_rows, axis=0).astype(BF16), e16_ref[...])
    ce = _dot_exact(jnp.concatenate([jnp.exp(cums[t]) for t in range(n_tok)], axis=0), e32_ref[...])

    for j in range(n_tok):
        r0 = (len(pairs) + j) * bb
        xh[j] = we[r0:r0 + bb] * xc[j, :, 0:SSD_D_INNER]

    dec = jnp.exp(cums[n_tok - 1])
    gw = SSD_D_INNER // SSD_N_GROUPS
    for b in range(bb):
        for g in range(SSD_N_GROUPS):
            cb_ = xc[:, b, SSD_D_INNER + SSD_GN + LANES * g:SSD_D_INNER + SSD_GN + LANES * (g + 1)]
            bb_ = xc[:, b, SSD_D_INNER + LANES * g:SSD_D_INNER + LANES * (g + 1)]
            h0 = sst_ref[b, gw * g:gw * (g + 1), :]
            yi[:, b, gw * g:gw * (g + 1)] = lax.dot_general(
                cb_.astype(BF16), h0.astype(BF16), _NT, preferred_element_type=F32)
            xhb = xh[:, b, gw * g:gw * (g + 1)]
            upd = lax.dot_general(xhb.astype(BF16), bb_.astype(BF16), _TN, preferred_element_type=F32)
            for r in range(rpg):
                hh = rpg * g + r
                d = dec[b:b + 1, hh:hh + 1]
                rows = slice(SSD_HEAD_DIM * r, SSD_HEAD_DIM * (r + 1))
                nsst_ref[b, gw * g + SSD_HEAD_DIM * r:gw * g + SSD_HEAD_DIM * (r + 1), :] = h0[rows] * d + upd[rows]

    for t in range(n_tok):
        xt = xc[t, :, 0:SSD_D_INNER]
        y = xt * drep_ref[...] + yi[t] * ce[t * bb:(t + 1) * bb]
        for j in range(t + 1):
            r0 = pairs.index((t, j)) * bb
            y = y + we[r0:r0 + bb] * xc[j, :, 0:SSD_D_INNER]
        y_ref[t] = y


def _ssd_sample_kernel(*refs, n_tok, bb, has_prev):
    if has_prev:
        refs = refs[:11] + refs[12:]
    nsst_ref = refs[13]
    fill = pl.program_id(0)

    @pl.when(fill == 0)
    def _():
        _ssd_sample_body(*refs, n_tok, bb)

    if not has_prev:
        @pl.when(fill > 0)
        def _():
            nsst_ref[...] = jnp.zeros_like(nsst_ref)


def _ssd_sample_scan(xbc, dt_raw, cst, sst_all, prev, small, e16, e32, layer, n_tok, batch):
    bb = SAMPLE_BB
    rows = SSD_N_HEADS * SSD_HEAD_DIM
    n_layers = sst_all.shape[0]
    n_blocks = batch // bb
    tb = lambda l, i: (0, _parked(i, l, n_blocks), 0)
    st_in, st_out, n_fill = _state_specs((bb, rows, SSD_D_STATE), layer, n_layers, prev, n_blocks)
    in_specs = [pl.BlockSpec((n_tok, bb, SSD_CONV_DIM), tb),
                pl.BlockSpec((n_tok, bb, SSD_DT_PAD), tb),
                pl.BlockSpec((SSD_CONV_W - 1, bb, SSD_CONV_DIM), tb),
                st_in] + _ssd_small_specs(layer) + [
                _resident((LANES, SSD_D_INNER)),
                _resident((LANES, SSD_D_INNER))]
    args = [xbc, dt_raw, cst, sst_all, *small, e16, e32]
    aliases = {}
    if prev is not None:
        in_specs.append(pl.BlockSpec(memory_space=pl.ANY))
        args.append(prev)
        aliases = {len(args) - 1: 2}
    return pl.pallas_call(
        functools.partial(_ssd_sample_kernel, n_tok=n_tok, bb=bb, has_prev=prev is not None),
        grid=(n_fill, n_blocks),
        in_specs=in_specs,
        out_specs=[pl.BlockSpec((n_tok, bb, SSD_D_INNER), tb),
                   pl.BlockSpec((SSD_CONV_W - 1, bb, SSD_CONV_DIM), tb),
                   st_out],
        out_shape=[jax.ShapeDtypeStruct((n_tok, batch, SSD_D_INNER), F32),
                   jax.ShapeDtypeStruct((SSD_CONV_W - 1, batch, SSD_CONV_DIM), F32),
                   jax.ShapeDtypeStruct((n_layers, batch, rows, SSD_D_STATE), F32)],
        scratch_shapes=[pltpu.VMEM((n_tok, bb, SSD_CONV_DIM), F32),
                        pltpu.VMEM((n_tok, bb, SSD_D_INNER), F32),
                        pltpu.VMEM((n_tok, bb, SSD_D_INNER), F32)],
        input_output_aliases=aliases,
        compiler_params=_params(2),
        name="ssd_sample_scan",
    )(*args)


def _hgrn_sample_body(q_ref, f_ref, v_ref, st_ref, lb_ref, lbc_ref, o_ref, nst_ref,
                      qe, kh, oi, n_tok, bb):
    nh = HGRN_N_HEADS
    qs, ks, cums = [], [], []
    run = None
    for t in range(n_tok):
        q, k, logf = _hgrn_gates(q_ref[t], f_ref[t], lb_ref[...], lbc_ref[...])
        run = logf if run is None else run + logf
        qs.append(q)
        ks.append(k)
        cums.append(run)
    last = cums[n_tok - 1]
    for t in range(n_tok):
        qe[t] = qs[t] * jnp.exp(cums[t])
        kh[t] = ks[t] * jnp.exp(last - cums[t])
    dec = jnp.exp(last)

    for h in range(nh):
        hs = slice(LANES * h, LANES * (h + 1))
        dec_t = dec[:, hs].T
        for b in range(bb):
            s0 = st_ref[b, h]
            oi[:, b, hs] = _dot(qe[:, b, hs].astype(BF16), s0.astype(BF16))
            upd = lax.dot_general(kh[:, b, hs].astype(BF16), v_ref[:, b, hs].astype(BF16), _TN,
                                  preferred_element_type=F32)
            nst_ref[b, h] = s0 * dec_t[:, b:b + 1] + upd

    for t in range(n_tok):
        o = oi[t]
        for j in range(t + 1):
            z = qs[t] * ks[j] * jnp.exp(cums[t] - cums[j])
            vj = v_ref[j]
            parts = []
            for h in range(nh):
                hs = slice(LANES * h, LANES * (h + 1))
                parts.append(jnp.sum(z[:, hs], axis=-1, keepdims=True) * vj[:, hs])
            o = o + jnp.concatenate(parts, axis=-1)
        o_ref[t] = o


def _hgrn_sample_kernel(*refs, n_tok, bb, has_prev):
    if has_prev:
        refs = refs[:6] + refs[7:]
    nst_ref = refs[7]
    fill = pl.program_id(0)

    @pl.when(fill == 0)
    def _():
        _hgrn_sample_body(*refs, n_tok, bb)

    if not has_prev:
        @pl.when(fill > 0)
        def _():
            nst_ref[...] = jnp.zeros_like(nst_ref)


def _hgrn_sample_scan(qr, fz, v, st_all, prev, lb_all, lbc_all, layer, n_tok, batch):
    bb = SAMPLE_BB
    n_layers = st_all.shape[0]
    n_blocks = batch // bb
    tb = lambda l, i: (0, _parked(i, l, n_blocks), 0)
    st_in, st_out, n_fill = _state_specs((bb, HGRN_N_HEADS, HGRN_DK, HGRN_DV), layer, n_layers, prev, n_blocks)
    in_specs = [pl.BlockSpec((n_tok, bb, HGRN_F), tb),
                pl.BlockSpec((n_tok, bb, HGRN_F), tb),
                pl.BlockSpec((n_tok, bb, D_MODEL), tb),
                st_in,
                _layer_resident((1, HGRN_F), layer),
                _layer_resident((1, HGRN_F), layer)]
    args = [qr, fz, v, st_all, lb_all, lbc_all]
    aliases = {}
    if prev is not None:
        in_specs.append(pl.BlockSpec(memory_space=pl.ANY))
        args.append(prev)
        aliases = {len(args) - 1: 1}
    return pl.pallas_call(
        functools.partial(_hgrn_sample_kernel, n_tok=n_tok, bb=bb, has_prev=prev is not None),
        grid=(n_fill, n_blocks),
        in_specs=in_specs,
        out_specs=[pl.BlockSpec((n_tok, bb, D_MODEL), tb), st_out],
        out_shape=[jax.ShapeDtypeStruct((n_tok, batch, D_MODEL), F32),
                   jax.ShapeDtypeStruct((n_layers, batch, HGRN_N_HEADS, HGRN_DK, HGRN_DV), F32)],
        scratch_shapes=[pltpu.VMEM((n_tok, bb, HGRN_F), F32),
                        pltpu.VMEM((n_tok, bb, HGRN_F), F32),
                        pltpu.VMEM((n_tok, bb, D_MODEL), F32)],
        input_output_aliases=aliases,
        compiler_params=_params(2),
        name="hgrn_sample_scan",
    )(*args)


def _prep_params(norm_mix_w, norm_mlp_w, norm_f_w, ssd_w_in, ssd_conv_w, ssd_conv_b, ssd_dt_bias,
                 ssd_a_log, ssd_d, ssd_norm_w, ssd_w_out, hgrn_w_in, hgrn_lb_raw, hgrn_norm_w,
                 hgrn_w_out, mlp_w_up, mlp_w_down):
    la = ssd_w_in.shape[0]
    pad_dt = SSD_DT_PAD - SSD_N_HEADS
    p = {}
    p["norm_mix"] = norm_mix_w.reshape(DEPTH, 1, D_MODEL)
    p["norm_mlp"] = norm_mlp_w.reshape(DEPTH, 1, D_MODEL)
    p["norm_f"] = norm_f_w.reshape(1, D_MODEL)
    p["ssd_w_in"] = jnp.pad(ssd_w_in, ((0, 0), (0, 0), (0, pad_dt))).astype(BF16)
    p["ssd_small"] = (
        ssd_conv_w,
        ssd_conv_b.reshape(la, 1, SSD_CONV_DIM),
        jnp.pad(ssd_dt_bias, ((0, 0), (0, pad_dt))).reshape(la, 1, SSD_DT_PAD),
        jnp.pad(-jnp.exp(ssd_a_log.astype(F32)), ((0, 0), (0, pad_dt))).reshape(la, 1, SSD_DT_PAD),
        jnp.repeat(ssd_d.astype(F32), SSD_HEAD_DIM, axis=1).reshape(la, 1, SSD_D_INNER))
    p["ssd_norm"] = ssd_norm_w.reshape(la, 1, SSD_D_INNER)
    p["ssd_w_out"] = ssd_w_out.astype(BF16)
    p["hgrn_w_in"] = hgrn_w_in.astype(BF16)
    sm = jax.nn.softmax(hgrn_lb_raw.astype(F32), axis=0)
    lb = jnp.cumsum(sm, axis=0) - sm[0]
    lbn = lb.shape[0]
    p["hgrn_lb"] = lb.reshape(lbn, 1, HGRN_F)
    p["hgrn_lbc"] = jnp.maximum(lb, LB_FLOOR).reshape(lbn, 1, HGRN_F)
    p["hgrn_norm"] = hgrn_norm_w.reshape(lbn, 1, D_MODEL)
    p["hgrn_w_out"] = hgrn_w_out.astype(BF16)
    p["mlp_w_up"] = mlp_w_up.astype(BF16)
    p["mlp_w_down"] = mlp_w_down.astype(BF16)
    head_of_lane = jnp.arange(SSD_D_INNER, dtype=jnp.int32) // SSD_HEAD_DIM
    expand = (jnp.arange(LANES, dtype=jnp.int32)[:, None] == head_of_lane[None, :])
    p["expand16"] = expand.astype(BF16)
    p["expand32"] = expand.astype(F32)
    return p


def _trunk(h, p, prompt, batch, seq, conv_states, ssm_states, hgrn_states):
    tm = min(TOKEN_TILE, h.shape[0])
    new_conv, new_ssm, new_hgrn = [], [], []
    ssm_all, hgrn_all = None, None
    for layer in range(DEPTH):
        j = layer // 2
        if layer % 2 == 0 and prompt:
            h, cst, sst = _ssd_prompt_layer(h, p, layer, j, batch, seq)
            new_conv.append(cst)
            new_ssm.append(sst.reshape(batch, SSD_N_HEADS, SSD_HEAD_DIM, SSD_D_STATE))
        elif layer % 2 == 0:
            z, xbc, dt_raw = _norm_matmul(h, p["norm_mix"], layer, p["ssd_w_in"], j,
                                          (SSD_D_INNER, SSD_CONV_DIM, SSD_DT_PAD), tm, "ssd_in_proj")
            n_l = ssm_states.shape[0]
            y, cst, ssm_all = _ssd_sample_scan(
                xbc.reshape(seq, batch, SSD_CONV_DIM), dt_raw.reshape(seq, batch, SSD_DT_PAD),
                jnp.swapaxes(conv_states[j], 0, 1),
                ssm_states.reshape(n_l, batch, SSD_N_HEADS * SSD_HEAD_DIM, SSD_D_STATE), ssm_all,
                p["ssd_small"], p["expand16"], p["expand32"], j, seq, batch)
            new_conv.append(jnp.swapaxes(cst, 0, 1))
            h = _gate_out(y.reshape(seq * batch, SSD_D_INNER), z, p["ssd_norm"], p["ssd_w_out"], j, h,
                          SSD_N_GROUPS, True, tm, "ssd_out_proj")
        elif prompt:
            h, st = _hgrn_prompt_layer(h, p, layer, j, batch, seq)
            new_hgrn.append(st)
        else:
            qr, fz, v, g = _norm_matmul(h, p["norm_mix"], layer, p["hgrn_w_in"], j,
                                        (HGRN_F, HGRN_F, D_MODEL, D_MODEL), tm, "hgrn_in_proj")
            o, hgrn_all = _hgrn_sample_scan(
                qr.reshape(seq, batch, HGRN_F), fz.reshape(seq, batch, HGRN_F),
                v.reshape(seq, batch, D_MODEL), hgrn_states, hgrn_all,
                p["hgrn_lb"], p["hgrn_lbc"], j, seq, batch)
            h = _gate_out(o.reshape(seq * batch, D_MODEL), g, p["hgrn_norm"], p["hgrn_w_out"], j, h,
                          HGRN_N_HEADS, False, tm, "hgrn_out_proj")
        h = _mlp(h, p["norm_mlp"], p["mlp_w_up"], p["mlp_w_down"], layer, p["norm_f"],
                 layer == DEPTH - 1, tm, "mlp")
    if prompt:
        return h, jnp.stack(new_conv), jnp.stack(new_ssm), jnp.stack(new_hgrn)
    n_l = ssm_states.shape[0]
    return (h, jnp.stack(new_conv),
            ssm_all.reshape(n_l, batch, SSD_N_HEADS, SSD_HEAD_DIM, SSD_D_STATE), hgrn_all)


def kernel(x_prompt, x_sample, state_ssd_conv, state_ssd_ssm, state_hgrn, norm_mix_w, norm_mlp_w, norm_f_w,
           ssd_w_in, ssd_conv_w, ssd_conv_b, ssd_dt_bias, ssd_a_log, ssd_d, ssd_norm_w, ssd_w_out,
           hgrn_w_in, hgrn_lb_raw, hgrn_norm_w, hgrn_w_out, mlp_w_up, mlp_w_down):
    p = _prep_params(norm_mix_w, norm_mlp_w, norm_f_w, ssd_w_in, ssd_conv_w, ssd_conv_b, ssd_dt_bias,
                     ssd_a_log, ssd_d, ssd_norm_w, ssd_w_out, hgrn_w_in, hgrn_lb_raw, hgrn_norm_w,
                     hgrn_w_out, mlp_w_up, mlp_w_down)
    bp, lp, d = x_prompt.shape
    bs, ls, _ = x_sample.shape
    y_p, conv_p, ssm_p, hgrn_p = _trunk(x_prompt.reshape(bp * lp, d), p, True, bp, lp, None, None, None)
    xs = jnp.swapaxes(x_sample, 0, 1).reshape(ls * bs, d)
    y_s, conv_s, ssm_s, hgrn_s = _trunk(xs, p, False, bs, ls, state_ssd_conv, state_ssd_ssm, state_hgrn)
    y_s = jnp.swapaxes(y_s.reshape(ls, bs, d), 0, 1)
    return (y_p.reshape(bp, lp, d), y_s, conv_p, ssm_p, hgrn_p, conv_s, ssm_s, hgrn_s)
```

```python
import functools

import jax
import jax.numpy as jnp
from jax import lax
from jax.experimental import pallas as pl
from jax.experimental.pallas import tpu as pltpu

F32 = jnp.float32
BF16 = jnp.bfloat16

D_MODEL = 1024
DEPTH = 4
SSD_D_INNER = 2048
SSD_HEAD_DIM = 64
SSD_N_HEADS = 32
SSD_N_GROUPS = 4
SSD_D_STATE = 128
SSD_CONV_W = 4
SSD_GN = SSD_N_GROUPS * SSD_D_STATE
SSD_CONV_DIM = SSD_D_INNER + 2 * SSD_GN
SSD_DT_PAD = 128
HGRN_N_HEADS = 8
HGRN_DK = 128
HGRN_DV = 128
HGRN_F = HGRN_N_HEADS * HGRN_DK
D_FF = 4 * D_MODEL
EPS = 1e-5
LB_FLOOR = 1e-20
NEG = -1e30
LANES = 128
SUBLANES = 8
VMEM_LIMIT = 56 * 1024 * 1024

TOKEN_TILE = 512
SSD_PROMPT_T = 256
SSD_PROMPT_Q = 128
HGRN_PROMPT_T = 256
HGRN_PROMPT_C = 64
HGRN_SUB = 8
HGRN_FAST_SUB = 32
HGRN_FAST_RANGE = 75.0
SAMPLE_BB = 8

_NT = (((1,), (1,)), ((), ()))
_TN = (((0,), (0,)), ((), ()))


def _sigmoid(x):
    return jax.nn.sigmoid(x)


def _silu(x):
    return x * jax.nn.sigmoid(x)


def _softplus(x):
    return jnp.maximum(x, 0.0) + jnp.log1p(jnp.exp(-jnp.abs(x)))


def _rms(x, w):
    ms = jnp.mean(x * x, axis=-1, keepdims=True)
    return x * lax.rsqrt(ms + EPS) * w


def _dot(a, b):
    return jnp.dot(a, b, preferred_element_type=F32)


def _dot_exact(a, b):
    return jnp.dot(a, b, precision=lax.Precision.HIGHEST, preferred_element_type=F32)


def _cumsum_rows(tril, x):
    hi = x.astype(BF16)
    r1 = x - hi.astype(F32)
    mid = r1.astype(BF16)
    lo = (r1 - mid.astype(F32)).astype(BF16)
    return _dot(tril, hi) + _dot(tril, mid) + _dot(tril, lo)


def _params(n_axes):
    return pltpu.CompilerParams(dimension_semantics=("arbitrary",) * n_axes,
                                vmem_limit_bytes=VMEM_LIMIT)


def _resident(shape):
    nd = len(shape)
    return pl.BlockSpec(shape, lambda *_: (0,) * nd, pipeline_mode=pl.Buffered(1))


def _layer_resident(shape, layer):
    nd = len(shape)
    return pl.BlockSpec((None,) + tuple(shape), lambda *_: (layer,) + (0,) * nd,
                        pipeline_mode=pl.Buffered(1))


def _norm_matmul_kernel(x_ref, nw_ref, w_ref, *o_refs):
    u = _rms(x_ref[...], nw_ref[...]).astype(BF16)
    c = 0
    for o_ref in o_refs:
        n = o_ref.shape[-1]
        o_ref[...] = _dot(u, w_ref[:, c:c + n])
        c += n


def _norm_matmul(x, nw_all, nw_layer, w_all, w_layer, splits, tm, name):
    m, d = x.shape
    n_tot = w_all.shape[-1]
    assert sum(splits) == n_tot and m % tm == 0
    return pl.pallas_call(
        _norm_matmul_kernel,
        grid=(m // tm,),
        in_specs=[pl.BlockSpec((tm, d), lambda i: (i, 0)),
                  _layer_resident((1, d), nw_layer),
                  _layer_resident((d, n_tot), w_layer)],
        out_specs=[pl.BlockSpec((tm, n), lambda i: (i, 0)) for n in splits],
        out_shape=[jax.ShapeDtypeStruct((m, n), F32) for n in splits],
        compiler_params=_params(1),
        name=name,
    )(x, nw_all, w_all)


def _gate_out_groups(y_ref, g_ref, nw_ref, w_ref, k0, k1, n_groups, gate_first):
    gs = y_ref.shape[-1] // n_groups
    parts = []
    for k in range(k0, k1):
        sl = slice(k * gs, (k + 1) * gs)
        yk = y_ref[:, sl]
        sg = _silu(g_ref[:, sl])
        if gate_first:
            yk = yk * sg
        ms = jnp.mean(yk * yk, axis=-1, keepdims=True)
        yn = yk * lax.rsqrt(ms + EPS) * nw_ref[:, sl]
        if not gate_first:
            yn = yn * sg
        parts.append(yn.astype(BF16))
    lhs = parts[0] if len(parts) == 1 else jnp.concatenate(parts, axis=1)
    return _dot(lhs, w_ref[k0 * gs:k1 * gs, :])


def _gate_out_compute(y_ref, g_ref, nw_ref, w_ref, acc, n_groups, gate_first):
    for k in range(n_groups):
        acc = acc + _gate_out_groups(y_ref, g_ref, nw_ref, w_ref, k, k + 1, n_groups, gate_first)
    return acc


def _gate_out_kernel(y_ref, g_ref, nw_ref, w_ref, h_ref, o_ref, *, n_groups, gate_first):
    o_ref[...] = _gate_out_compute(y_ref, g_ref, nw_ref, w_ref, h_ref[...], n_groups, gate_first)


def _gate_out(y, g, nw_all, w_all, layer, h, n_groups, gate_first, tm, name):
    m, di = y.shape
    d = h.shape[1]
    return pl.pallas_call(
        functools.partial(_gate_out_kernel, n_groups=n_groups, gate_first=gate_first),
        grid=(m // tm,),
        in_specs=[pl.BlockSpec((tm, di), lambda i: (i, 0)),
                  pl.BlockSpec((tm, di), lambda i: (i, 0)),
                  _layer_resident((1, di), layer),
                  _layer_resident((di, d), layer),
                  pl.BlockSpec((tm, d), lambda i: (i, 0))],
        out_specs=pl.BlockSpec((tm, d), lambda i: (i, 0)),
        out_shape=jax.ShapeDtypeStruct((m, d), F32),
        compiler_params=_params(1),
        name=name,
    )(y, g, nw_all, w_all, h)


def _mlp_kernel(x_ref, nw_ref, wu_ref, wd_ref, nf_ref, o_ref, *, final_norm, f_chunk):
    x = x_ref[...]
    u = _rms(x, nw_ref[...]).astype(BF16)
    acc = x
    for f in range(0, D_FF, f_chunk):
        mid = _dot(u, wu_ref[:, f:f + f_chunk])
        mid = jnp.square(jnp.maximum(mid, 0.0)).astype(BF16)
        acc = acc + _dot(mid, wd_ref[f:f + f_chunk, :])
    if final_norm:
        acc = _rms(acc, nf_ref[...])
    o_ref[...] = acc


def _mlp(x, nw_all, wu_all, wd_all, layer, nf, final_norm, tm, name):
    m, d = x.shape
    return pl.pallas_call(
        functools.partial(_mlp_kernel, final_norm=final_norm, f_chunk=1024),
        grid=(m // tm,),
        in_specs=[pl.BlockSpec((tm, d), lambda i: (i, 0)),
                  _layer_resident((1, d), layer),
                  _layer_resident((d, D_FF), layer),
                  _layer_resident((D_FF, d), layer),
                  _resident((1, d))],
        out_specs=pl.BlockSpec((tm, d), lambda i: (i, 0)),
        out_shape=jax.ShapeDtypeStruct((m, d), F32),
        compiler_params=_params(1),
        name=name,
    )(x, nw_all, wu_all, wd_all, nf)


def _ssd_small_specs(layer):
    return [_layer_resident((SSD_CONV_W, SSD_CONV_DIM), layer),
            _layer_resident((1, SSD_CONV_DIM), layer),
            _layer_resident((1, SSD_DT_PAD), layer),
            _layer_resident((1, SSD_DT_PAD), layer),
            _layer_resident((1, SSD_D_INNER), layer)]


def _ssd_project_jobs(h_tile, nmix_ref, win_ref, us, xext, zs, dts, t_len, n_blk=256):
    cdim = SSD_CONV_DIM
    us[...] = _rms(h_tile, nmix_ref[...]).astype(BF16)

    def job(dst, rows, c_dst, c_src, n):
        def run():
            dst[rows, c_dst:c_dst + n] = _dot(us[...], win_ref[:, c_src:c_src + n])
        return run

    every = slice(None)
    jobs = [job(dts, every, 0, SSD_D_INNER + cdim, SSD_DT_PAD)]
    jobs += [job(xext, slice(SUBLANES, SUBLANES + t_len), c, SSD_D_INNER + c, n_blk) for c in range(0, cdim, n_blk)]
    jobs += [job(zs, every, c, c, n_blk) for c in range(0, SSD_D_INNER, n_blk)]
    return jobs


def _ssd_tile(h_ref, xext, xext_next, zs, dts, cw_ref, cb_ref, dtb_ref, a_ref, drep_ref, nrm_ref, wout_ref,
              o_ref, rows_out, cst_ref, xc, ys, cq, ht, t_len, q_len, side_jobs):
    jobs = list(side_jobs)
    cdim = SSD_CONV_DIM
    off = SUBLANES - (SSD_CONV_W - 1)
    for c0 in range(0, cdim, 512):
        cs = slice(c0, c0 + 512)
        acc = cb_ref[:, cs] + xext[off:off + t_len, cs] * cw_ref[0:1, cs]
        for w in range(1, SSD_CONV_W):
            acc = acc + xext[off + w:off + w + t_len, cs] * cw_ref[w:w + 1, cs]
        xc[:, cs] = _silu(acc)
    cst_ref[...] = xext[t_len + off:t_len + SUBLANES, :]
    xext_next[0:SUBLANES, :] = xext[t_len:t_len + SUBLANES, :]

    dt_all = _softplus(dts[...] + dtb_ref[...])
    a_row = a_ref[...]
    ri = lax.broadcasted_iota(jnp.int32, (q_len, q_len), 0)
    ci = lax.broadcasted_iota(jnp.int32, (q_len, q_len), 1)
    causal = ri >= ci
    tril = jnp.where(causal, 1.0, 0.0).astype(BF16)
    lane = lax.broadcasted_iota(jnp.int32, (1, LANES), 1)
    lo_mask = lane < SSD_HEAD_DIM

    n_chunks = t_len // q_len
    for c in range(n_chunks):
        dtc = dt_all[c * q_len:(c + 1) * q_len]
        cum_c = _cumsum_rows(tril, dtc * a_row)
        cq[c, 0] = cum_c
        cq[c, 1] = cum_c.T
        cq[c, 2] = dtc.T

    o_ref[rows_out, :] = h_ref[rows_out, :]

    def out_proj_job(g):
        def run():
            o_ref[rows_out, :] += _gate_out_groups(ys, zs, nrm_ref, wout_ref, g, g + 1, SSD_N_GROUPS, True)
        return run

    for g in range(SSD_N_GROUPS):
        for c in range(n_chunks):
            rs = slice(c * q_len, (c + 1) * q_len)
            cum, cum_t, dt_t = cq.at[c, 0], cq.at[c, 1], cq.at[c, 2]
            b_g = xc[rs, SSD_D_INNER + LANES * g:SSD_D_INNER + LANES * (g + 1)]
            c_g = xc[rs, SSD_D_INNER + SSD_GN + LANES * g:SSD_D_INNER + SSD_GN + LANES * (g + 1)]
            cbm = lax.dot_general(c_g.astype(BF16), b_g.astype(BF16), _NT, preferred_element_type=F32)
            b_gt = b_g.T
            for m in range(SSD_N_HEADS // SSD_N_GROUPS // 2):
                p = g * 4 + m
                ps = slice(LANES * p, LANES * (p + 1))
                xp = xc[rs, ps]
                h_old = ht[p]
                w_mats, c_exps, b_ss, x_ss, h_ss, decs = [], [], [], [], [], []
                for s in range(2):
                    hh = 2 * p + s
                    msk = lo_mask if s == 0 else jnp.logical_not(lo_mask)
                    col = cum[:, hh:hh + 1]
                    row = cum_t[hh:hh + 1, :]
                    dt_row = dt_t[hh:hh + 1, :]
                    dec = jnp.exp(jnp.where(causal, col - row, NEG))
                    w_mats.append((cbm * dec * dt_row).astype(BF16))
                    c_exps.append((c_g * jnp.exp(col)).astype(BF16))
                    x_ss.append(jnp.where(msk, xp, 0.0).astype(BF16))
                    h_ss.append(jnp.where(msk, h_old, 0.0).astype(BF16))
                    last = cum[q_len - 1:q_len, hh:hh + 1]
                    b_ss.append((b_gt * (jnp.exp(last - row) * dt_row)).astype(BF16))
                    decs.append(jnp.exp(last))
                x_cat = jnp.concatenate(x_ss, axis=0)
                y_acc = (xp * drep_ref[:, ps]
                         + _dot(jnp.concatenate(w_mats + c_exps, axis=1),
                                jnp.concatenate([x_cat] + h_ss, axis=0)))
                dec_row = jnp.where(lo_mask, decs[0], decs[1])
                ht[p] = h_old * dec_row + _dot(jnp.concatenate(b_ss, axis=1), x_cat)
                ys[rs, ps] = y_acc
                if jobs:
                    jobs.pop(0)()
        jobs.insert(0, out_proj_job(g))

    while jobs:
        jobs.pop(0)()


def _ssd_prompt_kernel(h_ref, hn_ref, nmix_ref, win_ref, cw_ref, cb_ref, dtb_ref, a_ref, drep_ref, nrm_ref, wout_ref,
                       o_ref, cst_ref, sst_ref, us, xa, xb, za, zb, da, db, xc, ys, cq, ht, *, t_len, q_len):
    i = pl.program_id(1)
    rows_a, rows_b = slice(0, t_len), slice(t_len, 2 * t_len)
    rest = (cw_ref, cb_ref, dtb_ref, a_ref, drep_ref, nrm_ref, wout_ref)

    @pl.when((i == 0) & (pl.program_id(0) == 0))
    def _():
        for job in _ssd_project_jobs(h_ref[rows_a, :], nmix_ref, win_ref, us, xa, za, da, t_len):
            job()

    @pl.when(i == 0)
    def _():
        xa[0:SUBLANES, :] = jnp.zeros((SUBLANES, SSD_CONV_DIM), F32)
        ht[...] = jnp.zeros_like(ht)

    jobs_b = _ssd_project_jobs(h_ref[rows_b, :], nmix_ref, win_ref, us, xb, zb, db, t_len)
    _ssd_tile(h_ref, xa, xb, za, da, *rest, o_ref, rows_a, cst_ref, xc, ys, cq, ht, t_len, q_len, jobs_b)
    jobs_a = _ssd_project_jobs(hn_ref[...], nmix_ref, win_ref, us, xa, za, da, t_len)
    _ssd_tile(h_ref, xb, xa, zb, db, *rest, o_ref, rows_b, cst_ref, xc, ys, cq, ht, t_len, q_len, jobs_a)

    @pl.when(i == pl.num_programs(1) - 1)
    def _():
        for p in range(SSD_N_HEADS // 2):
            sst_ref[LANES * p:LANES * (p + 1), :] = ht[p].T


def _ssd_prompt_layer(h, p, layer, j, batch, seq):
    t_len, q_len = SSD_PROMPT_T, SSD_PROMPT_Q
    assert seq % (2 * t_len) == 0
    ns = seq // (2 * t_len)
    tok = lambda b, i: (b * ns + i, 0)
    nxt = lambda b, i: (2 * jnp.minimum(b * ns + i + 1, batch * ns - 1), 0)
    n_in = SSD_D_INNER + SSD_CONV_DIM + SSD_DT_PAD
    xbuf = pltpu.VMEM((t_len + SUBLANES, SSD_CONV_DIM), F32)
    zbuf = pltpu.VMEM((t_len, SSD_D_INNER), F32)
    dbuf = pltpu.VMEM((t_len, SSD_DT_PAD), F32)
    return pl.pallas_call(
        functools.partial(_ssd_prompt_kernel, t_len=t_len, q_len=q_len),
        grid=(batch, ns),
        in_specs=[pl.BlockSpec((2 * t_len, D_MODEL), tok),
                  pl.BlockSpec((t_len, D_MODEL), nxt),
                  _layer_resident((1, D_MODEL), layer),
                  _layer_resident((D_MODEL, n_in), j)] + _ssd_small_specs(j) + [
                  _layer_resident((1, SSD_D_INNER), j),
                  _layer_resident((SSD_D_INNER, D_MODEL), j)],
        out_specs=[pl.BlockSpec((2 * t_len, D_MODEL), tok),
                   pl.BlockSpec((None, SSD_CONV_W - 1, SSD_CONV_DIM), lambda b, i: (b, 0, 0)),
                   pl.BlockSpec((None, SSD_N_HEADS * SSD_HEAD_DIM, SSD_D_STATE), lambda b, i: (b, 0, 0))],
        out_shape=[jax.ShapeDtypeStruct((batch * seq, D_MODEL), F32),
                   jax.ShapeDtypeStruct((batch, SSD_CONV_W - 1, SSD_CONV_DIM), F32),
                   jax.ShapeDtypeStruct((batch, SSD_N_HEADS * SSD_HEAD_DIM, SSD_D_STATE), F32)],
        scratch_shapes=[pltpu.VMEM((t_len, D_MODEL), BF16), xbuf, xbuf, zbuf, zbuf, dbuf, dbuf,
                        pltpu.VMEM((t_len, SSD_CONV_DIM), F32),
                        pltpu.VMEM((t_len, SSD_D_INNER), F32),
                        pltpu.VMEM((t_len // q_len, 3, q_len, LANES), F32),
                        pltpu.VMEM((SSD_N_HEADS // 2, SSD_D_STATE, LANES), F32)],
        compiler_params=_params(2),
        name="ssd_prompt_layer",
    )(h, h, p["norm_mix"], p["ssd_w_in"], *p["ssd_small"], p["ssd_norm"], p["ssd_w_out"])


def _hgrn_gates(q_raw, fz, lb, lbc):
    e = jnp.exp(-jnp.abs(fz))
    r = 1.0 / (1.0 + e)
    er = e * r
    pos = fz >= 0.0
    sig_pos = jnp.where(pos, r, er)
    sig_neg = jnp.where(pos, er, r)
    logf = jnp.log(sig_pos + lbc * sig_neg)
    return _silu(q_raw), (1.0 - lb) * sig_neg, logf


def _hgrn_project_jobs(h_tile, nmix_ref, win_ref, lb_ref, lbc_ref, us, buf, t_len, c_len, n_blk=256):
    qs, ks, cs, vs, gs, ws = buf
    fsub = HGRN_FAST_SUB

    def gate_job(c0):
        def run():
            if c0 == 0:
                us[...] = _rms(h_tile(), nmix_ref[...]).astype(BF16)
            ri = lax.broadcasted_iota(jnp.int32, (c_len, c_len), 0)
            ci = lax.broadcasted_iota(jnp.int32, (c_len, c_len), 1)
            tril = jnp.where(ri >= ci, 1.0, 0.0).astype(BF16)
            cols = slice(c0, c0 + n_blk)
            fz = _dot(us[...], win_ref[:, HGRN_F + c0:HGRN_F + c0 + n_blk])
            qr = _dot(us[...], win_ref[:, c0:c0 + n_blk])
            q, k, logf = _hgrn_gates(qr, fz, lb_ref[:, cols], lbc_ref[:, cols])
            qs[:, cols] = q
            ks[:, cols] = k
            worst = None
            for c in range(t_len // c_len):
                cum = _cumsum_rows(tril, logf[c * c_len:(c + 1) * c_len])
                cs[c * c_len:(c + 1) * c_len, cols] = cum
                prev = None
                for a in range(c_len // fsub):
                    end = cum[(a + 1) * fsub - 1:(a + 1) * fsub, :]
                    drop = -end if prev is None else prev - end
                    worst = drop if worst is None else jnp.maximum(worst, drop)
                    prev = end
            w = worst[:, 0:LANES]
            for l0 in range(LANES, n_blk, LANES):
                w = jnp.maximum(w, worst[:, l0:l0 + LANES])
            ws[...] = w if c0 == 0 else jnp.maximum(ws[...], w)
        return run

    def plain_job(dst, c_src, c0):
        def run():
            dst[:, c0:c0 + n_blk] = _dot(us[...], win_ref[:, c_src + c0:c_src + c0 + n_blk])
        return run

    jobs = []
    for c0 in range(0, HGRN_F, n_blk):
        jobs += [gate_job(c0), plain_job(vs, 2 * HGRN_F, c0), plain_job(gs, 2 * HGRN_F + D_MODEL, c0)]
    return jobs


def _hgrn_tile(h_ref, rows, buf, nrm_ref, wout_ref, o_ref, os_, stt, t_len, c_len, side_jobs):
    qs, ks, cs, v_ref, g_ref, ws = buf
    nh = HGRN_N_HEADS
    n_chunks = t_len // c_len
    fsub = HGRN_FAST_SUB
    ri = lax.broadcasted_iota(jnp.int32, (c_len, c_len), 0)
    ci = lax.broadcasted_iota(jnp.int32, (c_len, c_len), 1)
    causal = ri >= ci
    fast = jnp.max(ws[...]) <= HGRN_FAST_RANGE

    o_ref[rows, :] = h_ref[rows, :]

    def out_proj_job(hp):
        def run():
            o_ref[rows, :] += _gate_out_groups(os_, g_ref, nrm_ref, wout_ref, 2 * hp, 2 * hp + 2, nh, False)
        return run

    row_c = lax.broadcasted_iota(jnp.int32, (c_len, LANES), 0)

    def chunk_head(h, hs, c, att_fn, o_extra):
        rs = slice(c * c_len, (c + 1) * c_len)
        qh, kh, vh, cum = qs[rs, hs], ks[rs, hs], v_ref[rs, hs], cs[rs, hs]
        st_old = stt[h]
        vb16 = vh.astype(BF16)
        o = lax.dot_general((qh * jnp.exp(cum)).astype(BF16), st_old.astype(BF16), _NT,
                            preferred_element_type=F32)
        o = o + _dot(att_fn(qh, kh, cum).astype(BF16), vb16)
        if o_extra is not None:
            o = o + o_extra[rs]
        os_[rs, hs] = o
        last = cum[c_len - 1:c_len, :]
        k_hat = (kh * jnp.exp(last - cum)).astype(BF16)
        upd = lax.dot_general(vb16, k_hat, _TN, preferred_element_type=F32)
        stt[h] = st_old * jnp.exp(last) + upd

    def cross_operands(qh, kh, cum, s):
        ref_rows = jnp.concatenate(
            [jnp.broadcast_to(cum[blk * 2 * s + s - 1:blk * 2 * s + s, :], (2 * s, LANES))
             for blk in range(c_len // (2 * s))], axis=0)
        second = (row_c // s) % 2 == 1
        e = jnp.exp(jnp.where(second, cum - ref_rows, ref_rows - cum))
        q_t = jnp.where(second, qh * e, 0.0).astype(BF16)
        k_t = jnp.where(second, 0.0, kh * e).astype(BF16)
        return q_t, k_t, s

    def cross_level(qh, kh, cum, s):
        q_t, k_t, _ = cross_operands(qh, kh, cum, s)
        part = lax.dot_general(q_t, k_t, _NT, preferred_element_type=F32)
        if 2 * s < c_len:
            part = jnp.where((ri // (2 * s)) == (ci // (2 * s)), part, 0.0)
        return part

    def levels_down_to(sub):
        out = []
        s = c_len // 2
        while s >= sub:
            out.append(s)
            s //= 2
        return out

    @pl.when(fast)
    def _():
        jobs = list(side_jobs)
        same_sub = causal & ((ri // fsub) == (ci // fsub))
        for h0 in range(0, nh, 4):
            heads = range(h0, h0 + 4)
            for c in range(n_chunks):
                rs = slice(c * c_len, (c + 1) * c_len)
                pre = {}
                for h in heads:
                    hs = slice(LANES * h, LANES * (h + 1))
                    qh, kh, cum = qs[rs, hs], ks[rs, hs], cs[rs, hs]
                    start = jnp.concatenate(
                        [jnp.zeros((fsub, LANES), F32)] +
                        [jnp.broadcast_to(cum[a * fsub - 1:a * fsub, :], (fsub, LANES))
                         for a in range(1, c_len // fsub)], axis=0)
                    last = cum[c_len - 1:c_len, :]
                    pre[h] = dict(
                        hs=hs, last=last, st_old=stt[h], vb16=v_ref[rs, hs].astype(BF16),
                        qe=(qh * jnp.exp(cum)).astype(BF16),
                        q_d=(qh * jnp.exp(cum - start)).astype(BF16),
                        k_d=(kh * jnp.exp(start - cum)).astype(BF16),
                        k_hat=(kh * jnp.exp(last - cum)).astype(BF16),
                        cross=[cross_operands(qh, kh, cum, s) for s in levels_down_to(fsub)])
                if jobs:
                    jobs.pop(0)()
                mm = {}
                for h in heads:
                    p = pre[h]
                    mm[h] = dict(
                        o=lax.dot_general(p["qe"], p["st_old"].astype(BF16), _NT, preferred_element_type=F32),
                        att=lax.dot_general(p["q_d"], p["k_d"], _NT, preferred_element_type=F32),
                        cross=[lax.dot_general(q_t, k_t, _NT, preferred_element_type=F32)
                               for (q_t, k_t, _) in p["cross"]],
                        upd=lax.dot_general(p["vb16"], p["k_hat"], _TN, preferred_element_type=F32))
                if jobs:
                    jobs.pop(0)()
                for h in heads:
                    p, m = pre[h], mm[h]
                    att = jnp.where(same_sub, m["att"], 0.0)
                    for part, (_, _, s) in zip(m["cross"], p["cross"]):
                        att = att + (jnp.where((ri // (2 * s)) == (ci // (2 * s)), part, 0.0)
                                     if 2 * s < c_len else part)
                    m["att16"] = att.astype(BF16)
                for h in heads:
                    p, m = pre[h], mm[h]
                    os_[rs, p["hs"]] = m["o"] + _dot(m["att16"], p["vb16"])
                    stt[h] = p["st_old"] * jnp.exp(p["last"]) + m["upd"]
                if jobs:
                    jobs.pop(0)()
            jobs[0:0] = [out_proj_job(hp) for hp in range(h0 // 2, h0 // 2 + 2)]
        while jobs:
            jobs.pop(0)()

    @pl.when(jnp.logical_not(fast))
    def _():
        for job in side_jobs:
            job()
        sub = HGRN_SUB
        row_in_sub = lax.broadcasted_iota(jnp.int32, (t_len, LANES), 0) % sub

        def att_general(qh, kh, cum):
            att = jnp.zeros((c_len, c_len), F32)
            for s in levels_down_to(sub):
                att = att + cross_level(qh, kh, cum, s)
            return att

        def head_body(h, carry):
            hs = pl.ds(pl.multiple_of(h * LANES, LANES), LANES)
            q_all = qs[:, hs]
            cum_all = cs[:, hs]
            o_diag = jnp.zeros((t_len, LANES), F32)
            for j in range(sub):
                def rows(ref):
                    return jnp.concatenate(
                        [jnp.broadcast_to(ref[sub * a + j:sub * a + j + 1, hs], (sub, LANES))
                         for a in range(t_len // sub)], axis=0)
                k_b, cum_b, v_b = rows(ks), rows(cs), rows(v_ref)
                arg = jnp.where(row_in_sub >= j, cum_all - cum_b, NEG)
                z = q_all * k_b * jnp.exp(arg)
                o_diag = o_diag + jnp.sum(z, axis=-1, keepdims=True) * v_b
            for c in range(n_chunks):
                chunk_head(h, hs, c, att_general, o_diag)
            return carry

        lax.fori_loop(0, nh, head_body, 0)
        for hp in range(nh // 2):
            out_proj_job(hp)()


def _hgrn_prompt_kernel(h_ref, hn_ref, nmix_ref, win_ref, lb_ref, lbc_ref, nrm_ref, wout_ref, o_ref, st_ref,
                        us, qa, ka, ca, va, ga, wa, qb, kb, cb, vb, gb, wb, os_, stt, *, t_len, c_len):
    i = pl.program_id(1)
    rows_a, rows_b = slice(0, t_len), slice(t_len, 2 * t_len)
    buf_a = (qa, ka, ca, va, ga, wa)
    buf_b = (qb, kb, cb, vb, gb, wb)
    proj = (nmix_ref, win_ref, lb_ref, lbc_ref, us)

    @pl.when((i == 0) & (pl.program_id(0) == 0))
    def _():
        for job in _hgrn_project_jobs(lambda: h_ref[rows_a, :], *proj, buf_a, t_len, c_len):
            job()

    @pl.when(i == 0)
    def _():
        stt[...] = jnp.zeros_like(stt)

    jobs_b = _hgrn_project_jobs(lambda: h_ref[rows_b, :], *proj, buf_b, t_len, c_len)
    _hgrn_tile(h_ref, rows_a, buf_a, nrm_ref, wout_ref, o_ref, os_, stt, t_len, c_len, jobs_b)
    jobs_a = _hgrn_project_jobs(lambda: hn_ref[...], *proj, buf_a, t_len, c_len)
    _hgrn_tile(h_ref, rows_b, buf_b, nrm_ref, wout_ref, o_ref, os_, stt, t_len, c_len, jobs_a)

    @pl.when(i == pl.num_programs(1) - 1)
    def _():
        for h in range(HGRN_N_HEADS):
            st_ref[h] = stt[h].T


def _hgrn_prompt_layer(h, p, layer, j, batch, seq):
    t_len, c_len = HGRN_PROMPT_T, HGRN_PROMPT_C
    assert seq % (2 * t_len) == 0
    ns = seq // (2 * t_len)
    tok = lambda b, i: (b * ns + i, 0)
    nxt = lambda b, i: (2 * jnp.minimum(b * ns + i + 1, batch * ns - 1), 0)
    n_in = 2 * HGRN_F + 2 * D_MODEL
    wide = pltpu.VMEM((t_len, HGRN_F), F32)
    worst = pltpu.VMEM((1, LANES), F32)
    return pl.pallas_call(
        functools.partial(_hgrn_prompt_kernel, t_len=t_len, c_len=c_len),
        grid=(batch, ns),
        in_specs=[pl.BlockSpec((2 * t_len, D_MODEL), tok),
                  pl.BlockSpec((t_len, D_MODEL), nxt),
                  _layer_resident((1, D_MODEL), layer),
                  _layer_resident((D_MODEL, n_in), j),
                  _layer_resident((1, HGRN_F), j),
                  _layer_resident((1, HGRN_F), j),
                  _layer_resident((1, D_MODEL), j),
                  _layer_resident((D_MODEL, D_MODEL), j)],
        out_specs=[pl.BlockSpec((2 * t_len, D_MODEL), tok),
                   pl.BlockSpec((None, HGRN_N_HEADS, HGRN_DK, HGRN_DV), lambda b, i: (b, 0, 0, 0))],
        out_shape=[jax.ShapeDtypeStruct((batch * seq, D_MODEL), F32),
                   jax.ShapeDtypeStruct((batch, HGRN_N_HEADS, HGRN_DK, HGRN_DV), F32)],
        scratch_shapes=[pltpu.VMEM((t_len, D_MODEL), BF16)] + [wide] * 5 + [worst] + [wide] * 5 + [worst] + [
                        pltpu.VMEM((t_len, D_MODEL), F32),
                        pltpu.VMEM((HGRN_N_HEADS, HGRN_DV, HGRN_DK), F32)],
        compiler_params=_params(2),
        name="hgrn_prompt_layer",
    )(h, h, p["norm_mix"], p["hgrn_w_in"], p["hgrn_lb"], p["hgrn_lbc"], p["hgrn_norm"], p["hgrn_w_out"])


def _parked(i, l, n_blocks):
    return jnp.where(l == 0, i, n_blocks - 1)


def _state_specs(block, layer, n_layers, prev, n_blocks):
    nd = len(block) - 1
    zeros = (0,) * nd
    in_spec = pl.BlockSpec((None,) + block, lambda l, i: (layer, _parked(i, l, n_blocks)) + zeros)
    if prev is None:
        out_spec = pl.BlockSpec((None,) + block, lambda l, i: (l, i) + zeros)
        return in_spec, out_spec, n_layers
    out_spec = pl.BlockSpec((None,) + block, lambda l, i: (layer, i) + zeros)
    return in_spec, out_spec, 1


def _ssd_sample_body(xbc_ref, dt_ref, cst_ref, sst_ref, cw_ref, cb_ref, dtb_ref, a_ref, drep_ref,
                     e16_ref, e32_ref, y_ref, ncst_ref, nsst_ref, xc, xh, yi, n_tok, bb):
    nw = SSD_CONV_W
    ext = [cst_ref[s] for s in range(nw - 1)] + [xbc_ref[t] for t in range(n_tok)]
    for t in range(n_tok):
        acc = cb_ref[...] + ext[t] * cw_ref[0:1, :]
        for w in range(1, nw):
            acc = acc + ext[t + w] * cw_ref[w:w + 1, :]
        xc[t] = _silu(acc)
    for s in range(nw - 1):
        ncst_ref[s] = ext[n_tok + s]

    a_row = a_ref[...]
    dts, cums = [], []
    run = None
    for t in range(n_tok):
        dt = _softplus(dt_ref[t] + dtb_ref[...])
        run = dt * a_row if run is None else run + dt * a_row
        dts.append(dt)
        cums.append(run)

    lane = lax.broadcasted_iota(jnp.int32, (bb, LANES), 1)
    rpg = SSD_N_HEADS // SSD_N_GROUPS

    def group_dot(t, j):
        out = None
        for g in reversed(range(SSD_N_GROUPS)):
            cg = xc[t, :, SSD_D_INNER + SSD_GN + LANES * g:SSD_D_INNER + SSD_GN + LANES * (g + 1)]
            bg = xc[j, :, SSD_D_INNER + LANES * g:SSD_D_INNER + LANES * (g + 1)]
            sg = jnp.sum(cg * bg, axis=-1, keepdims=True)
            out = jnp.broadcast_to(sg, (bb, LANES)) if out is None else jnp.where(lane < rpg * (g + 1), sg, out)
        return out

    pairs = [(t, j) for t in range(n_tok) for j in range(t + 1)]
    w_rows = [group_dot(t, j) * jnp.exp(cums[t] - cums[j]) * dts[j] for (t, j) in pairs]
    tail_rows = [jnp.exp(cums[n_tok - 1] - cums[j]) * dts[j] for j in range(n_tok)]
    we = _dot(jnp.concatenate(w_rows + tail_rows, axis=0).astype(BF16), e16_ref[...])
    ce = _dot_exact(jnp.concatenate([jnp.exp(cums[t]) for t in range(n_tok)], axis=0), e32_ref[...])

    for j in range(n_tok):
        r0 = (len(pairs) + j) * bb
        xh[j] = we[r0:r0 + bb] * xc[j, :, 0:SSD_D_INNER]

    dec = jnp.exp(cums[n_tok - 1])
    gw = SSD_D_INNER // SSD_N_GROUPS
    for b in range(bb):
        for g in range(SSD_N_GROUPS):
            cb_ = xc[:, b, SSD_D_INNER + SSD_GN + LANES * g:SSD_D_INNER + SSD_GN + LANES * (g + 1)]
            bb_ = xc[:, b, SSD_D_INNER + LANES * g:SSD_D_INNER + LANES * (g + 1)]
            h0 = sst_ref[b, gw * g:gw * (g + 1), :]
            yi[:, b, gw * g:gw * (g + 1)] = lax.dot_general(
                cb_.astype(BF16), h0.astype(BF16), _NT, preferred_element_type=F32)
            xhb = xh[:, b, gw * g:gw * (g + 1)]
            upd = lax.dot_general(xhb.astype(BF16), bb_.astype(BF16), _TN, preferred_element_type=F32)
            for r in range(rpg):
                hh = rpg * g + r
                d = dec[b:b + 1, hh:hh + 1]
                rows = slice(SSD_HEAD_DIM * r, SSD_HEAD_DIM * (r + 1))
                nsst_ref[b, gw * g + SSD_HEAD_DIM * r:gw * g + SSD_HEAD_DIM * (r + 1), :] = h0[rows] * d + upd[rows]

    for t in range(n_tok):
        xt = xc[t, :, 0:SSD_D_INNER]
        y = xt * drep_ref[...] + yi[t] * ce[t * bb:(t + 1) * bb]
        for j in range(t + 1):
            r0 = pairs.index((t, j)) * bb
            y = y + we[r0:r0 + bb] * xc[j, :, 0:SSD_D_INNER]
        y_ref[t] = y


def _ssd_sample_kernel(*refs, n_tok, bb, has_prev):
    if has_prev:
        refs = refs[:11] + refs[12:]
    nsst_ref = refs[13]
    fill = pl.program_id(0)

    @pl.when(fill == 0)
    def _():
        _ssd_sample_body(*refs, n_tok, bb)

    if not has_prev:
        @pl.when(fill > 0)
        def _():
            nsst_ref[...] = jnp.zeros_like(nsst_ref)


def _ssd_sample_scan(xbc, dt_raw, cst, sst_all, prev, small, e16, e32, layer, n_tok, batch):
    bb = SAMPLE_BB
    rows = SSD_N_HEADS * SSD_HEAD_DIM
    n_layers = sst_all.shape[0]
    n_blocks = batch // bb
    tb = lambda l, i: (0, _parked(i, l, n_blocks), 0)
    st_in, st_out, n_fill = _state_specs((bb, rows, SSD_D_STATE), layer, n_layers, prev, n_blocks)
    in_specs = [pl.BlockSpec((n_tok, bb, SSD_CONV_DIM), tb),
                pl.BlockSpec((n_tok, bb, SSD_DT_PAD), tb),
                pl.BlockSpec((SSD_CONV_W - 1, bb, SSD_CONV_DIM), tb),
                st_in] + _ssd_small_specs(layer) + [
                _resident((LANES, SSD_D_INNER)),
                _resident((LANES, SSD_D_INNER))]
    args = [xbc, dt_raw, cst, sst_all, *small, e16, e32]
    aliases = {}
    if prev is not None:
        in_specs.append(pl.BlockSpec(memory_space=pl.ANY))
        args.append(prev)
        aliases = {len(args) - 1: 2}
    return pl.pallas_call(
        functools.partial(_ssd_sample_kernel, n_tok=n_tok, bb=bb, has_prev=prev is not None),
        grid=(n_fill, n_blocks),
        in_specs=in_specs,
        out_specs=[pl.BlockSpec((n_tok, bb, SSD_D_INNER), tb),
                   pl.BlockSpec((SSD_CONV_W - 1, bb, SSD_CONV_DIM), tb),
                   st_out],
        out_shape=[jax.ShapeDtypeStruct((n_tok, batch, SSD_D_INNER), F32),
                   jax.ShapeDtypeStruct((SSD_CONV_W - 1, batch, SSD_CONV_DIM), F32),
                   jax.ShapeDtypeStruct((n_layers, batch, rows, SSD_D_STATE), F32)],
        scratch_shapes=[pltpu.VMEM((n_tok, bb, SSD_CONV_DIM), F32),
                        pltpu.VMEM((n_tok, bb, SSD_D_INNER), F32),
                        pltpu.VMEM((n_tok, bb, SSD_D_INNER), F32)],
        input_output_aliases=aliases,
        compiler_params=_params(2),
        name="ssd_sample_scan",
    )(*args)


def _hgrn_sample_body(q_ref, f_ref, v_ref, st_ref, lb_ref, lbc_ref, o_ref, nst_ref,
                      qe, kh, oi, n_tok, bb):
    nh = HGRN_N_HEADS
    qs, ks, cums = [], [], []
    run = None
    for t in range(n_tok):
        q, k, logf = _hgrn_gates(q_ref[t], f_ref[t], lb_ref[...], lbc_ref[...])
        run = logf if run is None else run + logf
        qs.append(q)
        ks.append(k)
        cums.append(run)
    last = cums[n_tok - 1]
    for t in range(n_tok):
        qe[t] = qs[t] * jnp.exp(cums[t])
        kh[t] = ks[t] * jnp.exp(last - cums[t])
    dec = jnp.exp(last)

    for h in range(nh):
        hs = slice(LANES * h, LANES * (h + 1))
        dec_t = dec[:, hs].T
        for b in range(bb):
            s0 = st_ref[b, h]
            oi[:, b, hs] = _dot(qe[:, b, hs].astype(BF16), s0.astype(BF16))
            upd = lax.dot_general(kh[:, b, hs].astype(BF16), v_ref[:, b, hs].astype(BF16), _TN,
                                  preferred_element_type=F32)
            nst_ref[b, h] = s0 * dec_t[:, b:b + 1] + upd

    for t in range(n_tok):
        o = oi[t]
        for j in range(t + 1):
            z = qs[t] * ks[j] * jnp.exp(cums[t] - cums[j])
            vj = v_ref[j]
            parts = []
            for h in range(nh):
                hs = slice(LANES * h, LANES * (h + 1))
                parts.append(jnp.sum(z[:, hs], axis=-1, keepdims=True) * vj[:, hs])
            o = o + jnp.concatenate(parts, axis=-1)
        o_ref[t] = o


def _hgrn_sample_kernel(*refs, n_tok, bb, has_prev):
    if has_prev:
        refs = refs[:6] + refs[7:]
    nst_ref = refs[7]
    fill = pl.program_id(0)

    @pl.when(fill == 0)
    def _():
        _hgrn_sample_body(*refs, n_tok, bb)

    if not has_prev:
        @pl.when(fill > 0)
        def _():
            nst_ref[...] = jnp.zeros_like(nst_ref)


def _hgrn_sample_scan(qr, fz, v, st_all, prev, lb_all, lbc_all, layer, n_tok, batch):
    bb = SAMPLE_BB
    n_layers = st_all.shape[0]
    n_blocks = batch // bb
    tb = lambda l, i: (0, _parked(i, l, n_blocks), 0)
    st_in, st_out, n_fill = _state_specs((bb, HGRN_N_HEADS, HGRN_DK, HGRN_DV), layer, n_layers, prev, n_blocks)
    in_specs = [pl.BlockSpec((n_tok, bb, HGRN_F), tb),
                pl.BlockSpec((n_tok, bb, HGRN_F), tb),
                pl.BlockSpec((n_tok, bb, D_MODEL), tb),
                st_in,
                _layer_resident((1, HGRN_F), layer),
                _layer_resident((1, HGRN_F), layer)]
    args = [qr, fz, v, st_all, lb_all, lbc_all]
    aliases = {}
    if prev is not None:
        in_specs.append(pl.BlockSpec(memory_space=pl.ANY))
        args.append(prev)
        aliases = {len(args) - 1: 1}
    return pl.pallas_call(
        functools.partial(_hgrn_sample_kernel, n_tok=n_tok, bb=bb, has_prev=prev is not None),
        grid=(n_fill, n_blocks),
        in_specs=in_specs,
        out_specs=[pl.BlockSpec((n_tok, bb, D_MODEL), tb), st_out],
        out_shape=[jax.ShapeDtypeStruct((n_tok, batch, D_MODEL), F32),
                   jax.ShapeDtypeStruct((n_layers, batch, HGRN_N_HEADS, HGRN_DK, HGRN_DV), F32)],
        scratch_shapes=[pltpu.VMEM((n_tok, bb, HGRN_F), F32),
                        pltpu.VMEM((n_tok, bb, HGRN_F), F32),
                        pltpu.VMEM((n_tok, bb, D_MODEL), F32)],
        input_output_aliases=aliases,
        compiler_params=_params(2),
        name="hgrn_sample_scan",
    )(*args)


def _prep_params(norm_mix_w, norm_mlp_w, norm_f_w, ssd_w_in, ssd_conv_w, ssd_conv_b, ssd_dt_bias,
                 ssd_a_log, ssd_d, ssd_norm_w, ssd_w_out, hgrn_w_in, hgrn_lb_raw, hgrn_norm_w,
                 hgrn_w_out, mlp_w_up, mlp_w_down):
    la = ssd_w_in.shape[0]
    pad_dt = SSD_DT_PAD - SSD_N_HEADS
    p = {}
    p["norm_mix"] = norm_mix_w.reshape(DEPTH, 1, D_MODEL)
    p["norm_mlp"] = norm_mlp_w.reshape(DEPTH, 1, D_MODEL)
    p["norm_f"] = norm_f_w.reshape(1, D_MODEL)
    p["ssd_w_in"] = jnp.pad(ssd_w_in, ((0, 0), (0, 0), (0, pad_dt))).astype(BF16)
    p["ssd_small"] = (
        ssd_conv_w,
        ssd_conv_b.reshape(la, 1, SSD_CONV_DIM),
        jnp.pad(ssd_dt_bias, ((0, 0), (0, pad_dt))).reshape(la, 1, SSD_DT_PAD),
        jnp.pad(-jnp.exp(ssd_a_log.astype(F32)), ((0, 0), (0, pad_dt))).reshape(la, 1, SSD_DT_PAD),
        jnp.repeat(ssd_d.astype(F32), SSD_HEAD_DIM, axis=1).reshape(la, 1, SSD_D_INNER))
    p["ssd_norm"] = ssd_norm_w.reshape(la, 1, SSD_D_INNER)
    p["ssd_w_out"] = ssd_w_out.astype(BF16)
    p["hgrn_w_in"] = hgrn_w_in.astype(BF16)
    sm = jax.nn.softmax(hgrn_lb_raw.astype(F32), axis=0)
    lb = jnp.cumsum(sm, axis=0) - sm[0]
    lbn = lb.shape[0]
    p["hgrn_lb"] = lb.reshape(lbn, 1, HGRN_F)
    p["hgrn_lbc"] = jnp.maximum(lb, LB_FLOOR).reshape(lbn, 1, HGRN_F)
    p["hgrn_norm"] = hgrn_norm_w.reshape(lbn, 1, D_MODEL)
    p["hgrn_w_out"] = hgrn_w_out.astype(BF16)
    p["mlp_w_up"] = mlp_w_up.astype(BF16)
    p["mlp_w_down"] = mlp_w_down.astype(BF16)
    head_of_lane = jnp.arange(SSD_D_INNER, dtype=jnp.int32) // SSD_HEAD_DIM
    expand = (jnp.arange(LANES, dtype=jnp.int32)[:, None] == head_of_lane[None, :])
    p["expand16"] = expand.astype(BF16)
    p["expand32"] = expand.astype(F32)
    return p


def _trunk(h, p, prompt, batch, seq, conv_states, ssm_states, hgrn_states):
    tm = min(TOKEN_TILE, h.shape[0])
    new_conv, new_ssm, new_hgrn = [], [], []
    ssm_all, hgrn_all = None, None
    for layer in range(DEPTH):
        j = layer // 2
        if layer % 2 == 0 and prompt:
            h, cst, sst = _ssd_prompt_layer(h, p, layer, j, batch, seq)
            new_conv.append(cst)
            new_ssm.append(sst.reshape(batch, SSD_N_HEADS, SSD_HEAD_DIM, SSD_D_STATE))
        elif layer % 2 == 0:
            z, xbc, dt_raw = _norm_matmul(h, p["norm_mix"], layer, p["ssd_w_in"], j,
                                          (SSD_D_INNER, SSD_CONV_DIM, SSD_DT_PAD), tm, "ssd_in_proj")
            n_l = ssm_states.shape[0]
            y, cst, ssm_all = _ssd_sample_scan(
                xbc.reshape(seq, batch, SSD_CONV_DIM), dt_raw.reshape(seq, batch, SSD_DT_PAD),
                jnp.swapaxes(conv_states[j], 0, 1),
                ssm_states.reshape(n_l, batch, SSD_N_HEADS * SSD_HEAD_DIM, SSD_D_STATE), ssm_all,
                p["ssd_small"], p["expand16"], p["expand32"], j, seq, batch)
            new_conv.append(jnp.swapaxes(cst, 0, 1))
            h = _gate_out(y.reshape(seq * batch, SSD_D_INNER), z, p["ssd_norm"], p["ssd_w_out"], j, h,
                          SSD_N_GROUPS, True, tm, "ssd_out_proj")
        elif prompt:
            h, st = _hgrn_prompt_layer(h, p, layer, j, batch, seq)
            new_hgrn.append(st)
        else:
            qr, fz, v, g = _norm_matmul(h, p["norm_mix"], layer, p["hgrn_w_in"], j,
                                        (HGRN_F, HGRN_F, D_MODEL, D_MODEL), tm, "hgrn_in_proj")
            o, hgrn_all = _hgrn_sample_scan(
                qr.reshape(seq, batch, HGRN_F), fz.reshape(seq, batch, HGRN_F),
                v.reshape(seq, batch, D_MODEL), hgrn_states, hgrn_all,
                p["hgrn_lb"], p["hgrn_lbc"], j, seq, batch)
            h = _gate_out(o.reshape(seq * batch, D_MODEL), g, p["hgrn_norm"], p["hgrn_w_out"], j, h,
                          HGRN_N_HEADS, False, tm, "hgrn_out_proj")
        h = _mlp(h, p["norm_mlp"], p["mlp_w_up"], p["mlp_w_down"], layer, p["norm_f"],
                 layer == DEPTH - 1, tm, "mlp")
    if prompt:
        return h, jnp.stack(new_conv), jnp.stack(new_ssm), jnp.stack(new_hgrn)
    n_l = ssm_states.shape[0]
    return (h, jnp.stack(new_conv),
            ssm_all.reshape(n_l, batch, SSD_N_HEADS, SSD_HEAD_DIM, SSD_D_STATE), hgrn_all)


def kernel(x_prompt, x_sample, state_ssd_conv, state_ssd_ssm, state_hgrn, norm_mix_w, norm_mlp_w, norm_f_w,
           ssd_w_in, ssd_conv_w, ssd_conv_b, ssd_dt_bias, ssd_a_log, ssd_d, ssd_norm_w, ssd_w_out,
           hgrn_w_in, hgrn_lb_raw, hgrn_norm_w, hgrn_w_out, mlp_w_up, mlp_w_down):
    p = _prep_params(norm_mix_w, norm_mlp_w, norm_f_w, ssd_w_in, ssd_conv_w, ssd_conv_b, ssd_dt_bias,
                     ssd_a_log, ssd_d, ssd_norm_w, ssd_w_out, hgrn_w_in, hgrn_lb_raw, hgrn_norm_w,
                     hgrn_w_out, mlp_w_up, mlp_w_down)
    bp, lp, d = x_prompt.shape
    bs, ls, _ = x_sample.shape
    y_p, conv_p, ssm_p, hgrn_p = _trunk(x_prompt.reshape(bp * lp, d), p, True, bp, lp, None, None, None)
    xs = jnp.swapaxes(x_sample, 0, 1).reshape(ls * bs, d)
    y_s, conv_s, ssm_s, hgrn_s = _trunk(xs, p, False, bs, ls, state_ssd_conv, state_ssd_ssm, state_hgrn)
    y_s = jnp.swapaxes(y_s.reshape(ls, bs, d), 0, 1)
    return (y_p.reshape(bp, lp, d), y_s, conv_p, ssm_p, hgrn_p, conv_s, ssm_s, hgrn_s)
```
